```python
import math
import jax, jax.numpy as jnp
from jax import lax
import numpy as np

D_MODEL = 2048
BATCH = 1
SEQ = 8192
DEPTH = 1
DEC_BATCH = 32
DEC_SEQ = 1
PAST_LEN = 8192
PAGE_SIZE = 128

N_HEADS = 16
HEAD_DIM = 64
D_ATTN = N_HEADS * HEAD_DIM
MOBA_BLOCK = 256
MOBA_TOPK = 3
Q_CHUNK = 128
POOL_WINDOWS = (2, 4, 8, 16)
N_POOL_GROUPS = len(POOL_WINDOWS)
POOL_GROUP_DIM = 256
D_POOL = N_POOL_GROUPS * POOL_GROUP_DIM
POOL_OUT_GROUP = D_MODEL // N_POOL_GROUPS
POOL_STATE = max(POOL_WINDOWS) - 1
N_BUCKETS = 32
MAX_DISTANCE = 128
D_FF = -(-8 * D_MODEL // (3 * 256)) * 256
D_IN = D_POOL + 3 * D_ATTN + 2 * D_MODEL
ALPHA = (2.0 * DEPTH) ** 0.25
BETA = (8.0 * DEPTH) ** -0.25
LN_EPS = 1e-5

kernel_name = 'hybrid_pool_moba_gated_decoder_step'


def layer_norm(x, g, b):
    xf = x.astype(jnp.float32)
    mu = jnp.mean(xf, axis=-1, keepdims=True)
    var = jnp.mean(jnp.square(xf - mu), axis=-1, keepdims=True)
    return ((xf - mu) * lax.rsqrt(var + LN_EPS) * g + b).astype(x.dtype)


def t5_bucket(rel):
    n = jnp.maximum(rel, 0)
    max_exact = N_BUCKETS // 2
    nf = jnp.maximum(n, 1).astype(jnp.float32)
    large = max_exact + (jnp.log(nf / max_exact) / math.log(MAX_DISTANCE / max_exact)
                         * (N_BUCKETS - max_exact)).astype(jnp.int32)
    large = jnp.minimum(large, N_BUCKETS - 1)
    return jnp.where(n < max_exact, n, large)


def pool_mix(u_ext, n_ctx, pos0):
    b, l, _ = u_ext.shape
    n = l - n_ctx
    uf = u_ext.astype(jnp.float32).reshape(b, l, N_POOL_GROUPS, POOL_GROUP_DIM)
    cs = jnp.concatenate([jnp.zeros_like(uf[:, :1]), jnp.cumsum(uf, axis=1)], axis=1)
    t = n_ctx + jnp.arange(n)
    pos = pos0 + jnp.arange(n)
    groups = []
    for gi, w in enumerate(POOL_WINDOWS):
        start = jnp.maximum(t + 1 - w, 0)
        win_sum = cs[:, t + 1, gi] - cs[:, start, gi]
        cnt = jnp.minimum(pos + 1, w).astype(jnp.float32)
        groups.append(win_sum / cnt[None, :, None])
    return jnp.stack(groups, axis=2) - uf[:, n_ctx:]


def moba_attend(q, q_pos, kb, vb, kmean, rel_bias):
    b, n, h, dh = q.shape
    nb = kb.shape[1]
    top = min(MOBA_TOPK, nb)
    cur = q_pos // MOBA_BLOCK
    gate = jnp.einsum('bnhd,bjhd->bnhj', q.astype(jnp.float32), kmean)
    fully_past = jnp.arange(nb)[None, None, None, :] < cur[None, :, None, None]
    gate = jnp.where(fully_past, gate, -jnp.inf)
    _, sel = lax.top_k(gate, top)
    sel_valid = sel < cur[None, :, None, None]
    own = jnp.broadcast_to(cur[None, :, None, None], (b, n, h, 1))
    idx = jnp.concatenate([sel, own], axis=-1)
    valid_blk = jnp.concatenate([sel_valid, jnp.ones((b, n, h, 1), bool)], axis=-1)
    bi = jnp.arange(b)[:, None, None, None]
    hi = jnp.arange(h)[None, None, :, None]
    kg = kb[bi, idx, :, hi, :]
    vg = vb[bi, idx, :, hi, :]
    kpos = idx[..., None] * MOBA_BLOCK + jnp.arange(MOBA_BLOCK)
    qp = q_pos[None, :, None, None, None]
    is_own = (jnp.arange(top + 1) == top)[:, None]
    valid = jnp.where(is_own, kpos <= qp, valid_blk[..., None])
    logits = jnp.einsum('bnhd,bnhskd->bnhsk', q, kg,
                        preferred_element_type=jnp.float32) * (1.0 / math.sqrt(dh))
    bias = rel_bias[t5_bucket(qp - kpos), hi[..., None]].astype(jnp.float32)
    logits = jnp.where(valid, logits + bias, -jnp.inf)
    p = jax.nn.softmax(logits.reshape(b, n, h, -1), axis=-1).reshape(logits.shape)
    out = jnp.einsum('bnhsk,bnhskd->bnhd', p, vg.astype(jnp.float32))
    return out.astype(q.dtype)


def hybrid_layer(x, pool_ctx, past_k, past_v, rel_bias, w_in, w_pool, pool_scale,
                 w_attn_out, w_o, ln1_g, ln1_b, w_ffn_in, w_ffn_out, ln2_g, ln2_b):
    b, n, _ = x.shape
    pos0 = past_k.shape[1]
    z = x @ w_in
    u, q, k, v, g = jnp.split(z, [D_POOL, D_POOL + D_ATTN, D_POOL + 2 * D_ATTN,
                                  D_POOL + 3 * D_ATTN], axis=-1)
    u_ext = jnp.concatenate([pool_ctx, u], axis=1)
    pooled = pool_mix(u_ext, pool_ctx.shape[1], pos0)
    p_br = jnp.einsum('bngc,gce->bnge', pooled.astype(x.dtype), w_pool)
    p_br = p_br.reshape(b, n, D_MODEL) * pool_scale
    q = q.reshape(b, n, N_HEADS, HEAD_DIM)
    k = k.reshape(b, n, N_HEADS, HEAD_DIM)
    v = v.reshape(b, n, N_HEADS, HEAD_DIM)
    total = pos0 + n
    pad = -(-total // MOBA_BLOCK) * MOBA_BLOCK - total
    zpad = jnp.zeros((b, pad, N_HEADS, HEAD_DIM), x.dtype)
    k_all = jnp.concatenate([past_k.astype(x.dtype), k, zpad], axis=1)
    v_all = jnp.concatenate([past_v.astype(x.dtype), v, zpad], axis=1)
    nb = k_all.shape[1] // MOBA_BLOCK
    kb = k_all.reshape(b, nb, MOBA_BLOCK, N_HEADS, HEAD_DIM)
    vb = v_all.reshape(b, nb, MOBA_BLOCK, N_HEADS, HEAD_DIM)
    kmean = jnp.mean(kb.astype(jnp.float32), axis=2)
    q_pos = pos0 + jnp.arange(n, dtype=jnp.int32)
    if n > Q_CHUNK and n % Q_CHUNK == 0:
        nc = n // Q_CHUNK
        qc = q.reshape(b, nc, Q_CHUNK, N_HEADS, HEAD_DIM).swapaxes(0, 1)
        pc = q_pos.reshape(nc, Q_CHUNK)
        a = lax.map(lambda qp: moba_attend(qp[0], qp[1], kb, vb, kmean, rel_bias), (qc, pc))
        a = a.swapaxes(0, 1).reshape(b, n, D_ATTN)
    else:
        a = moba_attend(q, q_pos, kb, vb, kmean, rel_bias).reshape(b, n, D_ATTN)
    a_br = a @ w_attn_out
    g_pool, g_attn = jnp.split(jax.nn.sigmoid(g), 2, axis=-1)
    mix = (g_pool * p_br + g_attn * a_br) @ w_o
    h1 = layer_norm(ALPHA * x + mix, ln1_g, ln1_b)
    f_gate, f_up = jnp.split(h1 @ w_ffn_in, 2, axis=-1)
    ffn = (jax.nn.silu(f_gate) * f_up) @ w_ffn_out
    h2 = layer_norm(ALPHA * h1 + ffn, ln2_g, ln2_b)
    new_pool = u_ext[:, -POOL_STATE:]
    return h2, k, v, new_pool


def setup_inputs(seed: int = 0) -> dict:
    key = jax.random.key(seed)
    ks = jax.random.split(key, 20)
    f32 = jnp.float32
    n_pages = PAST_LEN // PAGE_SIZE
    n_used = DEC_BATCH * n_pages
    n_phys = n_used + n_used // 4

    def nrm(k, shape, scale):
        return jax.random.normal(k, shape, f32) * scale

    x_prompt = nrm(ks[0], (BATCH, SEQ, D_MODEL), 1.0)
    x_sample = nrm(ks[1], (DEC_BATCH, DEC_SEQ, D_MODEL), 1.0)
    cache_k = nrm(ks[2], (DEPTH, n_phys, PAGE_SIZE, N_HEADS, HEAD_DIM), 1.0)
    cache_v = nrm(ks[3], (DEPTH, n_phys, PAGE_SIZE, N_HEADS, HEAD_DIM), BETA)
    state_pool = nrm(ks[4], (DEPTH, DEC_BATCH, POOL_STATE, D_POOL), 1.0)
    page_table = jax.random.permutation(ks[5], n_phys)[:n_used].reshape(
        DEC_BATCH, n_pages).astype(jnp.int32)
    rel_bias = nrm(ks[6], (N_BUCKETS, N_HEADS), 0.5)
    col_scale = jnp.concatenate([jnp.ones((D_POOL + 2 * D_ATTN,), f32),
                                 jnp.full((D_ATTN,), BETA, f32),
                                 jnp.ones((2 * D_MODEL,), f32)])
    w_in = nrm(ks[7], (DEPTH, D_MODEL, D_IN), D_MODEL ** -0.5) * col_scale
    w_pool = nrm(ks[8], (DEPTH, N_POOL_GROUPS, POOL_GROUP_DIM, POOL_OUT_GROUP),
                 POOL_GROUP_DIM ** -0.5)
    pool_scale = 1.0 + nrm(ks[9], (DEPTH, D_MODEL), 0.1)
    w_attn_out = nrm(ks[10], (DEPTH, D_ATTN, D_MODEL), D_ATTN ** -0.5)
    w_o = nrm(ks[11], (DEPTH, D_MODEL, D_MODEL), D_MODEL ** -0.5 * BETA)
    ln1_g = 1.0 + nrm(ks[12], (DEPTH, D_MODEL), 0.05)
    ln1_b = nrm(ks[13], (DEPTH, D_MODEL), 0.02)
    w_ffn_in = nrm(ks[14], (DEPTH, D_MODEL, 2 * D_FF), D_MODEL ** -0.5 * BETA)
    w_ffn_out = nrm(ks[15], (DEPTH, D_FF, D_MODEL), D_FF ** -0.5 * BETA)
    ln2_g = 1.0 + nrm(ks[16], (DEPTH, D_MODEL), 0.05)
    ln2_b = nrm(ks[17], (DEPTH, D_MODEL), 0.02)
    return {'x_prompt': x_prompt, 'x_sample': x_sample, 'cache_k': cache_k,
            'cache_v': cache_v, 'state_pool': state_pool, 'page_table': page_table,
            'rel_bias': rel_bias, 'w_in': w_in, 'w_pool': w_pool, 'pool_scale': pool_scale,
            'w_attn_out': w_attn_out, 'w_o': w_o, 'ln1_g': ln1_g, 'ln1_b': ln1_b,
            'w_ffn_in': w_ffn_in, 'w_ffn_out': w_ffn_out, 'ln2_g': ln2_g, 'ln2_b': ln2_b}


def reference(x_prompt, x_sample, cache_k, cache_v, state_pool, page_table, rel_bias,
              w_in, w_pool, pool_scale, w_attn_out, w_o, ln1_g, ln1_b,
              w_ffn_in, w_ffn_out, ln2_g, ln2_b):
    hp, hs = x_prompt, x_sample
    bp = x_prompt.shape[0]
    bs = x_sample.shape[0]
    empty_pool = jnp.zeros((bp, 0, D_POOL), x_prompt.dtype)
    empty_kv = jnp.zeros((bp, 0, N_HEADS, HEAD_DIM), x_prompt.dtype)
    kp_l, vp_l, ks_l, vs_l, sp_l, ss_l = [], [], [], [], [], []
    for l in range(DEPTH):
        params = (rel_bias, w_in[l], w_pool[l], pool_scale[l], w_attn_out[l], w_o[l],
                  ln1_g[l], ln1_b[l], w_ffn_in[l], w_ffn_out[l], ln2_g[l], ln2_b[l])
        hp, kp, vp, sp = hybrid_layer(hp, empty_pool, empty_kv, empty_kv, *params)
        past_k = cache_k[l][page_table].reshape(bs, -1, N_HEADS, HEAD_DIM)
        past_v = cache_v[l][page_table].reshape(bs, -1, N_HEADS, HEAD_DIM)
        hs, ks_, vs_, ss = hybrid_layer(hs, state_pool[l], past_k, past_v, *params)
        kp_l.append(kp); vp_l.append(vp); ks_l.append(ks_); vs_l.append(vs_)
        sp_l.append(sp); ss_l.append(ss)
    return (hp, hs, jnp.stack(kp_l), jnp.stack(vp_l), jnp.stack(ks_l), jnp.stack(vs_l),
            jnp.stack(sp_l), jnp.stack(ss_l))
```

```python
import functools
import math

import numpy as np
import jax
import jax.numpy as jnp
from jax import lax
from jax.experimental import pallas as pl
from jax.experimental.pallas import tpu as pltpu

D_MODEL = 2048
N_HEADS = 16
HEAD_DIM = 64
D_ATTN = N_HEADS * HEAD_DIM
MOBA_BLOCK = 256
MOBA_TOPK = 3
POOL_WINDOWS = (2, 4, 8, 16)
N_POOL_GROUPS = len(POOL_WINDOWS)
POOL_GROUP_DIM = 256
D_POOL = N_POOL_GROUPS * POOL_GROUP_DIM
POOL_OUT_GROUP = D_MODEL // N_POOL_GROUPS
POOL_STATE = max(POOL_WINDOWS) - 1
POOL_HALO = 16
N_BUCKETS = 32
MAX_DISTANCE = 128
D_FF = -(-8 * D_MODEL // (3 * 256)) * 256
ALPHA = 2.0 ** 0.25
LN_EPS = 1e-5
NEG_BIG = -1e30
V_ROWS = 80
VMEM_LIMIT = 56 * 1024 * 1024

F32 = jnp.float32
BF16 = jnp.bfloat16


def _t5_bucket_np(rel):
    n = np.maximum(rel, 0)
    max_exact = N_BUCKETS // 2
    nf = np.maximum(n, 1).astype(np.float32)
    large = max_exact + (np.log(nf / max_exact) / math.log(MAX_DISTANCE / max_exact)
                         * (N_BUCKETS - max_exact)).astype(np.int32)
    large = np.minimum(large, N_BUCKETS - 1)
    return np.where(n < max_exact, n, large)


_BUCKETS = _t5_bucket_np(np.arange(0, 4 * MOBA_BLOCK))
BUCKET_START = tuple(int(np.argmax(_BUCKETS >= t)) for t in range(N_BUCKETS))
FAR_REL = BUCKET_START[N_BUCKETS - 1]
assert FAR_REL <= MOBA_BLOCK


def _params(*sem):
    return pltpu.CompilerParams(dimension_semantics=sem, vmem_limit_bytes=VMEM_LIMIT)


def _layer_norm(y, g, b):
    mu = jnp.mean(y, axis=-1, keepdims=True)
    d = y - mu
    var = jnp.mean(d * d, axis=-1, keepdims=True)
    return d * lax.rsqrt(var + LN_EPS) * g + b


def _proj_kernel(x_ref, w_ref, *out_refs, epilogue):
    z = jnp.dot(x_ref[...], w_ref[...], preferred_element_type=F32)
    epilogue(z, *out_refs)


def _proj(xb, wb, col0, n_cols, tm, tn, out_dtypes, epilogue, extra_out=None):
    m, k = xb.shape
    assert m % tm == 0 and n_cols % tn == 0 and col0 % tn == 0
    grid = (n_cols // tn, m // tm)
    out_shape = [jax.ShapeDtypeStruct((m, n_cols), dt) for dt in out_dtypes]
    out_specs = [pl.BlockSpec((tm, tn), lambda j, i: (i, j)) for _ in out_dtypes]
    if extra_out is not None:
        out_shape.append(extra_out[0])
        out_specs.append(extra_out[1])
    return pl.pallas_call(
        functools.partial(_proj_kernel, epilogue=epilogue),
        grid=grid,
        in_specs=[pl.BlockSpec((tm, k), lambda j, i: (i, 0)),
                  pl.BlockSpec((k, tn), lambda j, i: (0, col0 // tn + j))],
        out_specs=out_specs,
        out_shape=out_shape,
        compiler_params=_params("arbitrary", "arbitrary"),
    )(xb, wb)


def _ep_f32(z, o_ref):
    o_ref[...] = z


def _ep_q(z, o_ref):
    o_ref[...] = (z * (1.0 / math.sqrt(HEAD_DIM))).astype(o_ref.dtype)


def _ep_kv(z, o_ref, ob_ref):
    o_ref[...] = z
    ob_ref[...] = z.astype(BF16)


def _ep_k_mean(z, o_ref, ob_ref, km_ref):
    o_ref[...] = z
    ob_ref[...] = z.astype(BF16)
    tm, tn = z.shape
    km_ref[0] = jnp.mean(z.reshape(tm // MOBA_BLOCK, MOBA_BLOCK, tn), axis=1)


def _ep_gate(z, o_ref):
    o_ref[...] = jax.nn.sigmoid(z).astype(BF16)


def _pool_prompt_kernel(u_ref, halo_ref, o_ref):
    i = pl.program_id(0)
    tm = u_ref.shape[0]
    u = u_ref[...]
    halo = jnp.where(i > 0, halo_ref[...], 0.0)
    ext = jnp.concatenate([halo, u], axis=0)
    pos = i * tm + lax.broadcasted_iota(jnp.int32, (tm, 1), 0)
    for g, w in enumerate(POOL_WINDOWS):
        cols = slice(g * POOL_GROUP_DIM, (g + 1) * POOL_GROUP_DIM)
        s = ext[:, cols]
        span = 1
        while span < w:
            s = s + pltpu.roll(s, span, axis=0)
            span *= 2
        cnt = jnp.minimum(pos + 1, w).astype(F32)
        o_ref[:, cols] = (s[POOL_HALO:] / cnt - u[:, cols]).astype(o_ref.dtype)


def _pool_prompt(u, tm):
    m = u.shape[0]
    hb = tm // POOL_HALO
    return pl.pallas_call(
        _pool_prompt_kernel,
        grid=(m // tm,),
        in_specs=[pl.BlockSpec((tm, D_POOL), lambda i: (i, 0)),
                  pl.BlockSpec((POOL_HALO, D_POOL), lambda i: (jnp.maximum(i * hb - 1, 0), 0))],
        out_specs=pl.BlockSpec((tm, D_POOL), lambda i: (i, 0)),
        out_shape=jax.ShapeDtypeStruct((m, D_POOL), BF16),
        compiler_params=_params("arbitrary"),
    )(u, u)


def _pool_sample_kernel(st_ref, u_ref, o_ref):
    st = st_ref[...]
    u = u_ref[...]
    t = lax.broadcasted_iota(jnp.int32, (1, POOL_STATE, D_POOL), 1)
    col = lax.broadcasted_iota(jnp.int32, (1, POOL_STATE, D_POOL), 2)
    colv = lax.broadcasted_iota(jnp.int32, (1, D_POOL), 1)
    first = jnp.zeros_like(t)
    wv = jnp.zeros((1, D_POOL), F32)
    for g, w in enumerate(POOL_WINDOWS):
        first = jnp.where(col // POOL_GROUP_DIM == g, POOL_STATE - (w - 1), first)
        wv = jnp.where(colv // POOL_GROUP_DIM == g, float(w), wv)
    win = jnp.sum(jnp.where(t >= first, st, 0.0), axis=1) + u
    o_ref[...] = (win / wv - u).astype(o_ref.dtype)


def _pool_sample(state, u):
    b = u.shape[0]
    return pl.pallas_call(
        _pool_sample_kernel,
        out_shape=jax.ShapeDtypeStruct((b, D_POOL), BF16),
        compiler_params=pltpu.CompilerParams(vmem_limit_bytes=VMEM_LIMIT),
    )(state, u)


def _mix_kernel(pooled_ref, a_ref, gp_ref, ga_ref, x_ref, wpool_ref, pscale_ref, wao_ref,
                wo_ref, g_ref, b_ref, h_ref, hb_ref):
    parts = [jnp.dot(pooled_ref[:, g * POOL_GROUP_DIM:(g + 1) * POOL_GROUP_DIM], wpool_ref[g],
                     preferred_element_type=F32) for g in range(N_POOL_GROUPS)]
    p_br = jnp.concatenate(parts, axis=-1) * pscale_ref[...]
    a_br = jnp.dot(a_ref[...], wao_ref[...], preferred_element_type=F32)
    merged = gp_ref[...].astype(F32) * p_br + ga_ref[...].astype(F32) * a_br
    mix = jnp.dot(merged.astype(BF16), wo_ref[...], preferred_element_type=F32)
    h = _layer_norm(ALPHA * x_ref[...] + mix, g_ref[...], b_ref[...])
    h_ref[...] = h
    hb_ref[...] = h.astype(BF16)


def _mix(pooled, a, gates, x, wpool, pscale, wao, wo, ln_g, ln_b, tm):
    m = x.shape[0]
    row = lambda i: (i, 0)
    const2 = lambda i: (0, 0)
    return pl.pallas_call(
        _mix_kernel,
        grid=(m // tm,),
        in_specs=[pl.BlockSpec((tm, D_POOL), row),
                  pl.BlockSpec((tm, D_ATTN), row),
                  pl.BlockSpec((tm, D_MODEL), lambda i: (i, 0)),
                  pl.BlockSpec((tm, D_MODEL), lambda i: (i, 1)),
                  pl.BlockSpec((tm, D_MODEL), row),
                  pl.BlockSpec((N_POOL_GROUPS, POOL_GROUP_DIM, POOL_OUT_GROUP), lambda i: (0, 0, 0)),
                  pl.BlockSpec((1, D_MODEL), const2),
                  pl.BlockSpec((D_ATTN, D_MODEL), const2),
                  pl.BlockSpec((D_MODEL, D_MODEL), const2),
                  pl.BlockSpec((1, D_MODEL), const2),
                  pl.BlockSpec((1, D_MODEL), const2)],
        out_specs=[pl.BlockSpec((tm, D_MODEL), row), pl.BlockSpec((tm, D_MODEL), row)],
        out_shape=[jax.ShapeDtypeStruct((m, D_MODEL), F32), jax.ShapeDtypeStruct((m, D_MODEL), BF16)],
        compiler_params=_params("arbitrary"),
    )(pooled, a, gates, gates, x, wpool, pscale, wao, wo, ln_g, ln_b)


def _ffn_kernel(hb_ref, h_ref, wg_ref, wu_ref, wo_ref, g_ref, b_ref, o_ref, acc_ref):
    f = pl.program_id(1)

    @pl.when(f == 0)
    def _():
        acc_ref[...] = jnp.zeros_like(acc_ref)

    hb = hb_ref[...]
    gate = jnp.dot(hb, wg_ref[...], preferred_element_type=F32)
    up = jnp.dot(hb, wu_ref[...], preferred_element_type=F32)
    act = (gate * jax.nn.sigmoid(gate) * up).astype(BF16)
    acc_ref[...] += jnp.dot(act, wo_ref[...], preferred_element_type=F32)

    @pl.when(f == pl.num_programs(1) - 1)
    def _():
        o_ref[...] = _layer_norm(ALPHA * h_ref[...] + acc_ref[...], g_ref[...], b_ref[...])


def _ffn(hb, h, w_in, w_out, ln_g, ln_b, tm, tf):
    m = h.shape[0]
    nf = D_FF // tf
    row = lambda i, f: (i, 0)
    const2 = lambda i, f: (0, 0)
    return pl.pallas_call(
        _ffn_kernel,
        grid=(m // tm, nf),
        in_specs=[pl.BlockSpec((tm, D_MODEL), row),
                  pl.BlockSpec((tm, D_MODEL), row),
                  pl.BlockSpec((D_MODEL, tf), lambda i, f: (0, f)),
                  pl.BlockSpec((D_MODEL, tf), lambda i, f: (0, nf + f)),
                  pl.BlockSpec((tf, D_MODEL), lambda i, f: (f, 0)),
                  pl.BlockSpec((1, D_MODEL), const2),
                  pl.BlockSpec((1, D_MODEL), const2)],
        out_specs=pl.BlockSpec((tm, D_MODEL), row),
        out_shape=jax.ShapeDtypeStruct((m, D_MODEL), F32),
        scratch_shapes=[pltpu.VMEM((tm, D_MODEL), F32)],
        compiler_params=_params("arbitrary", "arbitrary"),
    )(hb, h, w_in, w_in, w_out, ln_g, ln_b)


def _bias_tile(relb_ref, h, rel):
    b = jnp.full(rel.shape, relb_ref[0, h], F32)
    for t in range(1, N_BUCKETS):
        b = jnp.where(rel >= BUCKET_START[t], relb_ref[t, h], b)
    return b


def _attn_kernel(relb_ref, qT_ref, k_ref, vT_ref, km_ref, oT_ref, bown_ref, bprev_ref, sel_ref):
    h = pl.program_id(0)
    c = pl.program_id(1)
    nb = k_ref.shape[1]
    blk = MOBA_BLOCK

    @pl.when(c == 0)
    def _():
        kk = lax.broadcasted_iota(jnp.int32, (blk, blk), 0)
        qq = lax.broadcasted_iota(jnp.int32, (blk, blk), 1)
        rel = qq - kk
        bown_ref[...] = jnp.where(rel >= 0, _bias_tile(relb_ref, h, rel), NEG_BIG)
        bprev_ref[...] = _bias_tile(relb_ref, h, rel + blk)

    qT = qT_ref[0]

    km = km_ref[0]
    km_hi = km.astype(BF16)
    r1 = km - km_hi.astype(F32)
    km_mid = r1.astype(BF16)
    km_lo = (r1 - km_mid.astype(F32)).astype(BF16)
    score = (jnp.dot(km_hi, qT, preferred_element_type=F32)
             + jnp.dot(km_mid, qT, preferred_element_type=F32)
             + jnp.dot(km_lo, qT, preferred_element_type=F32))
    jidx = lax.broadcasted_iota(jnp.int32, (nb, blk), 0).astype(F32)
    past = jidx < c.astype(F32)
    s_left = jnp.where(past, score, -jnp.inf)
    chosen = jnp.zeros((nb, blk), F32)
    for _ in range(min(MOBA_TOPK, nb)):
        mx = jnp.max(s_left, axis=0, keepdims=True)
        first = jnp.min(jnp.where(s_left == mx, jidx, float(nb)), axis=0, keepdims=True)
        hit = jidx == first
        chosen = jnp.where(hit, 1.0, chosen)
        s_left = jnp.where(hit, -jnp.inf, s_left)
    sel_ref[...] = jnp.where(past, chosen, 0.0)

    s = jnp.dot(k_ref[0, c], qT, preferred_element_type=F32) + bown_ref[...]
    m = jnp.max(s, axis=0, keepdims=True)
    p = jnp.exp(s - m)
    acc = jnp.dot(vT_ref[0, c], p.astype(BF16), preferred_element_type=F32)

    def visit(j, s, m, acc):
        keep = sel_ref[pl.ds(j, 1), :] > 0.5
        m_j = jnp.where(keep, jnp.max(s, axis=0, keepdims=True), NEG_BIG)
        m_new = jnp.maximum(m, m_j)
        alpha = jnp.exp(m - m_new)
        p = jnp.exp(s - m_new)
        pv = jnp.dot(vT_ref[0, j], p.astype(BF16), preferred_element_type=F32)
        return m_new, alpha * acc + jnp.where(keep, pv, 0.0)

    jp = jnp.maximum(c - 1, 0)
    s = jnp.dot(k_ref[0, jp], qT, preferred_element_type=F32) + bprev_ref[...]
    m, acc = visit(jp, s, m, acc)

    far_bias = relb_ref[N_BUCKETS - 1, h]

    def far_body(j, carry):
        m, acc = carry
        s = jnp.dot(k_ref[0, j], qT, preferred_element_type=F32) + far_bias
        return visit(j, s, m, acc)

    m, acc = lax.fori_loop(0, jnp.maximum(c - 1, 0), far_body, (m, acc))

    oT_ref[0] = (acc[:HEAD_DIM] / acc[HEAD_DIM:HEAD_DIM + 1]).astype(oT_ref.dtype)


def _attn_prompt(rel_bias, qT, k4, vT4, km3):
    nh, _, n = qT.shape
    nb = n // MOBA_BLOCK
    return pl.pallas_call(
        _attn_kernel,
        grid_spec=pltpu.PrefetchScalarGridSpec(
            num_scalar_prefetch=0,
            grid=(nh, nb),
            in_specs=[pl.BlockSpec(memory_space=pltpu.SMEM),
                      pl.BlockSpec((1, HEAD_DIM, MOBA_BLOCK), lambda h, c: (h, 0, c)),
                      pl.BlockSpec((1, nb, MOBA_BLOCK, HEAD_DIM), lambda h, c: (h, 0, 0, 0)),
                      pl.BlockSpec((1, nb, V_ROWS, MOBA_BLOCK), lambda h, c: (h, 0, 0, 0)),
                      pl.BlockSpec((1, nb, HEAD_DIM), lambda h, c: (h, 0, 0))],
            out_specs=pl.BlockSpec((1, HEAD_DIM, MOBA_BLOCK), lambda h, c: (h, 0, c)),
            scratch_shapes=[pltpu.VMEM((MOBA_BLOCK, MOBA_BLOCK), F32),
                            pltpu.VMEM((MOBA_BLOCK, MOBA_BLOCK), F32),
                            pltpu.VMEM((nb, MOBA_BLOCK), F32)]),
        out_shape=jax.ShapeDtypeStruct((nh, HEAD_DIM, n), BF16),
        compiler_params=_params("arbitrary", "arbitrary"),
    )(rel_bias, qT, k4, vT4, km3)


def _cache_scan_kernel(pt_ref, k0_ref, k1_ref, v0_ref, v1_ref, q_ref, bias_ref,
                       o_ref, km_ref, m_ref, l_ref):
    q = q_ref[0][None]
    bias = bias_ref[0]
    half = k0_ref.shape[2]
    ks = (k0_ref[0, 0], k1_ref[0, 0])
    vs = (v0_ref[0, 0], v1_ref[0, 0])
    ss = [jnp.sum(k * q, axis=-1, keepdims=True) + bias[i * half:(i + 1) * half]
          for i, k in enumerate(ks)]
    m = jnp.maximum(jnp.max(ss[0], axis=0), jnp.max(ss[1], axis=0))
    ps = [jnp.exp(s - m[None]) for s in ss]
    l = jnp.sum(ps[0], axis=0) + jnp.sum(ps[1], axis=0)
    o = jnp.sum(ps[0] * vs[0], axis=0) + jnp.sum(ps[1] * vs[1], axis=0)
    o_ref[0, 0] = o
    km_ref[0, 0] = (jnp.sum(ks[0], axis=0) + jnp.sum(ks[1], axis=0)) * (1.0 / (2 * half))
    m_ref[0, 0] = jnp.broadcast_to(m, m_ref.shape[2:])
    l_ref[0, 0] = jnp.broadcast_to(l, l_ref.shape[2:])


def _cache_scan(page_table, cache_k, cache_v, q3, bias_s):
    b, n_pages = page_table.shape
    page = cache_k.shape[2]
    assert MOBA_BLOCK == 2 * page
    nb = n_pages // 2
    pg = (1, 1, page, N_HEADS, HEAD_DIM)
    stat = jax.ShapeDtypeStruct((b, nb, N_HEADS, 128), F32)
    part = jax.ShapeDtypeStruct((b, nb, N_HEADS, HEAD_DIM), F32)
    return pl.pallas_call(
        _cache_scan_kernel,
        grid_spec=pltpu.PrefetchScalarGridSpec(
            num_scalar_prefetch=1,
            grid=(b, nb),
            in_specs=[pl.BlockSpec(pg, lambda s, j, pt: (0, pt[s, 2 * j], 0, 0, 0)),
                      pl.BlockSpec(pg, lambda s, j, pt: (0, pt[s, 2 * j + 1], 0, 0, 0)),
                      pl.BlockSpec(pg, lambda s, j, pt: (0, pt[s, 2 * j], 0, 0, 0)),
                      pl.BlockSpec(pg, lambda s, j, pt: (0, pt[s, 2 * j + 1], 0, 0, 0)),
                      pl.BlockSpec((1, N_HEADS, HEAD_DIM), lambda s, j, pt: (s, 0, 0)),
                      pl.BlockSpec((1, MOBA_BLOCK, N_HEADS, 1),
                                   lambda s, j, pt: ((j + 1) // nb, 0, 0, 0))],
            out_specs=[pl.BlockSpec((1, 1, N_HEADS, HEAD_DIM), lambda s, j, pt: (s, j, 0, 0)),
                       pl.BlockSpec((1, 1, N_HEADS, HEAD_DIM), lambda s, j, pt: (s, j, 0, 0)),
                       pl.BlockSpec((1, 1, N_HEADS, 128), lambda s, j, pt: (s, j, 0, 0)),
                       pl.BlockSpec((1, 1, N_HEADS, 128), lambda s, j, pt: (s, j, 0, 0))]),
        out_shape=[part, part, stat, stat],
        compiler_params=_params("arbitrary", "arbitrary"),
    )(page_table, cache_k, cache_k, cache_v, cache_v, q3, bias_s)


def _decode_combine_kernel(o_ref, km_ref, m_ref, l_ref, q_ref, kn_ref, vn_ref, b0_ref, a_ref):
    q = q_ref[0]
    km = km_ref[0]
    nb = km.shape[0]
    score = jnp.sum(km * q[None], axis=-1, keepdims=True)
    jidx = lax.broadcasted_iota(jnp.int32, score.shape, 0).astype(F32)
    chosen = jnp.zeros(score.shape, F32)
    s_left = score
    for _ in range(min(MOBA_TOPK, nb)):
        mx = jnp.max(s_left, axis=0, keepdims=True)
        first = jnp.min(jnp.where(s_left == mx, jidx, float(nb)), axis=0, keepdims=True)
        hit = jidx == first
        chosen = jnp.where(hit, 1.0, chosen)
        s_left = jnp.where(hit, -jnp.inf, s_left)
    keep = chosen > 0.5
    m3 = m_ref[0][:, :, 0:1]
    l3 = l_ref[0][:, :, 0:1]
    s_own = jnp.sum(kn_ref[0] * q, axis=-1, keepdims=True) + b0_ref[:, 0:1]
    m_all = jnp.maximum(jnp.max(jnp.where(keep, m3, -jnp.inf), axis=0), s_own)
    w = jnp.where(keep, jnp.exp(m3 - m_all[None]), 0.0)
    w_own = jnp.exp(s_own - m_all)
    num = jnp.sum(w * o_ref[0], axis=0) + w_own * vn_ref[0]
    den = jnp.sum(w * l3, axis=0) + w_own
    a_ref[0] = num / den


def _decode_combine(o, km, m, l, q3, kn3, vn3, b0):
    b, nb = o.shape[:2]
    part = pl.BlockSpec((1, nb, N_HEADS, HEAD_DIM), lambda s: (s, 0, 0, 0))
    stat = pl.BlockSpec((1, nb, N_HEADS, 128), lambda s: (s, 0, 0, 0))
    tok = pl.BlockSpec((1, N_HEADS, HEAD_DIM), lambda s: (s, 0, 0))
    return pl.pallas_call(
        _decode_combine_kernel,
        grid=(b,),
        in_specs=[part, part, stat, stat, tok, tok, tok,
                  pl.BlockSpec((N_HEADS, 128), lambda s: (0, 0))],
        out_specs=tok,
        out_shape=jax.ShapeDtypeStruct((b, N_HEADS, HEAD_DIM), F32),
        compiler_params=_params("arbitrary"),
    )(o, km, m, l, q3, kn3, vn3, b0)


def kernel(x_prompt, x_sample, cache_k, cache_v, state_pool, page_table, rel_bias, w_in, w_pool,
           pool_scale, w_attn_out, w_o, ln1_g, ln1_b, w_ffn_in, w_ffn_out, ln2_g, ln2_b):
    assert w_in.shape[0] == 1 and x_prompt.shape[0] == 1 and x_sample.shape[1] == 1
    n = x_prompt.shape[1]
    bs = x_sample.shape[0]
    nb = n // MOBA_BLOCK
    past_len = page_table.shape[1] * cache_k.shape[2]
    assert past_len % MOBA_BLOCK == 0 and past_len >= max(POOL_WINDOWS)

    w_in_b = w_in[0].astype(BF16)
    w_pool_b = w_pool[0].astype(BF16)
    w_ao_b = w_attn_out[0].astype(BF16)
    w_o_b = w_o[0].astype(BF16)
    w_fi_b = w_ffn_in[0].astype(BF16)
    w_fo_b = w_ffn_out[0].astype(BF16)
    pscale = pool_scale[0][None]
    g1, b1, g2, b2 = ln1_g[0][None], ln1_b[0][None], ln2_g[0][None], ln2_b[0][None]

    xp = x_prompt[0]
    xs = x_sample[:, 0]
    xp_b = xp.astype(BF16)
    xs_b = xs.astype(BF16)
    col_q, col_k, col_v, col_g = D_POOL, D_POOL + D_ATTN, D_POOL + 2 * D_ATTN, D_POOL + 3 * D_ATTN

    tm, tn = 1024, 1024
    (u_p,) = _proj(xp_b, w_in_b, 0, D_POOL, tm, tn, [F32], _ep_f32)
    (q_p,) = _proj(xp_b, w_in_b, col_q, D_ATTN, tm, tn, [BF16], _ep_q)
    km_extra = (jax.ShapeDtypeStruct((n // tm, tm // MOBA_BLOCK, D_ATTN), F32),
                pl.BlockSpec((1, tm // MOBA_BLOCK, tn), lambda j, i: (i, 0, j)))
    k_p, kb_p, km_p = _proj(xp_b, w_in_b, col_k, D_ATTN, tm, tn, [F32, BF16], _ep_k_mean,
                            extra_out=km_extra)
    v_p, vb_p = _proj(xp_b, w_in_b, col_v, D_ATTN, tm, tn, [F32, BF16], _ep_kv)
    (gates_p,) = _proj(xp_b, w_in_b, col_g, 2 * D_MODEL, tm, tn, [BF16], _ep_gate)

    qT = q_p.reshape(n, N_HEADS, HEAD_DIM).transpose(1, 2, 0)
    k4 = kb_p.reshape(nb, MOBA_BLOCK, N_HEADS, HEAD_DIM).transpose(2, 0, 1, 3)
    vT4 = vb_p.reshape(nb, MOBA_BLOCK, N_HEADS, HEAD_DIM).transpose(2, 0, 3, 1)
    ones_rows = jnp.concatenate([jnp.ones((N_HEADS, nb, 1, MOBA_BLOCK), BF16),
                                 jnp.zeros((N_HEADS, nb, V_ROWS - HEAD_DIM - 1, MOBA_BLOCK), BF16)],
                                axis=2)
    vT4 = jnp.concatenate([vT4, ones_rows], axis=2)
    km3 = km_p.reshape(nb, N_HEADS, HEAD_DIM).transpose(1, 0, 2)
    aT = _attn_prompt(rel_bias, qT, k4, vT4, km3)
    a_p = aT.reshape(D_ATTN, n).T

    pooled_p = _pool_prompt(u_p, 512)
    h1_p, h1b_p = _mix(pooled_p, a_p, gates_p, xp, w_pool_b, pscale, w_ao_b, w_o_b, g1, b1, 256)
    y_p = _ffn(h1b_p, h1_p, w_fi_b, w_fo_b, g2, b2, 512, 512)

    (u_s,) = _proj(xs_b, w_in_b, 0, D_POOL, bs, tn, [F32], _ep_f32)
    (q_s,) = _proj(xs_b, w_in_b, col_q, D_ATTN, bs, tn, [F32], _ep_q)
    (k_s,) = _proj(xs_b, w_in_b, col_k, D_ATTN, bs, tn, [F32], _ep_f32)
    (v_s,) = _proj(xs_b, w_in_b, col_v, D_ATTN, bs, tn, [F32], _ep_f32)
    (gates_s,) = _proj(xs_b, w_in_b, col_g, 2 * D_MODEL, bs, tn, [BF16], _ep_gate)

    rel_last = past_len - (past_len - MOBA_BLOCK + np.arange(MOBA_BLOCK))
    bias_last = rel_bias[_t5_bucket_np(rel_last)]
    bias_far = jnp.broadcast_to(rel_bias[N_BUCKETS - 1], bias_last.shape)
    assert past_len - (past_len - 2 * MOBA_BLOCK + MOBA_BLOCK - 1) >= FAR_REL
    bias_s = jnp.stack([bias_far, bias_last])[..., None]
    q3 = q_s.reshape(bs, N_HEADS, HEAD_DIM)
    o_b, km_b, m_b, l_b = _cache_scan(page_table, cache_k, cache_v, q3, bias_s)
    b0 = jnp.broadcast_to(rel_bias[0][:, None], (N_HEADS, 128))
    a_s = _decode_combine(o_b, km_b, m_b, l_b, q3, k_s.reshape(bs, N_HEADS, HEAD_DIM),
                          v_s.reshape(bs, N_HEADS, HEAD_DIM), b0)
    a_s = a_s.reshape(bs, D_ATTN).astype(BF16)

    pooled_s = _pool_sample(state_pool[0], u_s)
    h1_s, h1b_s = _mix(pooled_s, a_s, gates_s, xs, w_pool_b, pscale, w_ao_b, w_o_b, g1, b1, bs)
    y_s = _ffn(h1b_s, h1_s, w_fi_b, w_fo_b, g2, b2, bs, 512)

    hd = (N_HEADS, HEAD_DIM)
    new_pool_p = u_p[n - POOL_STATE:][None, None]
    new_pool_s = jnp.concatenate([state_pool[0][:, 1:], u_s[:, None]], axis=1)[None]
    return (y_p[None], y_s[:, None],
            k_p.reshape(1, 1, n, *hd), v_p.reshape(1, 1, n, *hd),
            k_s.reshape(1, bs, 1, *hd), v_s.reshape(1, bs, 1, *hd),
            new_pool_p, new_pool_s)
```

```python
import functools
import math

import numpy as np
import jax
import jax.numpy as jnp
from jax import lax
from jax.experimental import pallas as pl
from jax.experimental.pallas import tpu as pltpu

D_MODEL = 2048
N_HEADS = 16
HEAD_DIM = 64
D_ATTN = N_HEADS * HEAD_DIM
MOBA_BLOCK = 256
MOBA_TOPK = 3
POOL_WINDOWS = (2, 4, 8, 16)
N_POOL_GROUPS = len(POOL_WINDOWS)
POOL_GROUP_DIM = 256
D_POOL = N_POOL_GROUPS * POOL_GROUP_DIM
POOL_OUT_GROUP = D_MODEL // N_POOL_GROUPS
POOL_STATE = max(POOL_WINDOWS) - 1
POOL_HALO = 16
N_BUCKETS = 32
MAX_DISTANCE = 128
D_FF = -(-8 * D_MODEL // (3 * 256)) * 256
ALPHA = 2.0 ** 0.25
LN_EPS = 1e-5
NEG_BIG = -1e30
POS_BIG = 1e30
LANES = 128
V_ROWS = 80
ATTN_UNROLL = 4
ATTN_HEADS = 2
LOG2E = 1.4426950408889634
SCAN_BLOCKS = 8
VMEM_LIMIT = 56 * 1024 * 1024

F32 = jnp.float32
BF16 = jnp.bfloat16


def _t5_bucket_np(rel):
    n = np.maximum(rel, 0)
    max_exact = N_BUCKETS // 2
    nf = np.maximum(n, 1).astype(np.float32)
    large = max_exact + (np.log(nf / max_exact) / math.log(MAX_DISTANCE / max_exact)
                         * (N_BUCKETS - max_exact)).astype(np.int32)
    large = np.minimum(large, N_BUCKETS - 1)
    return np.where(n < max_exact, n, large)


_BUCKETS = _t5_bucket_np(np.arange(0, 4 * MOBA_BLOCK))
BUCKET_START = tuple(int(np.argmax(_BUCKETS >= t)) for t in range(N_BUCKETS))
FAR_REL = BUCKET_START[N_BUCKETS - 1]
assert FAR_REL <= MOBA_BLOCK


def _params(*sem):
    return pltpu.CompilerParams(dimension_semantics=sem, vmem_limit_bytes=VMEM_LIMIT)


def _layer_norm(y, g, b):
    mu = jnp.mean(y, axis=-1, keepdims=True)
    d = y - mu
    var = jnp.mean(d * d, axis=-1, keepdims=True)
    return d * lax.rsqrt(var + LN_EPS) * g + b


def _bias_tile(relb_ref, h, rel):
    b = jnp.full(rel.shape, relb_ref[0, h], F32)
    for t in range(1, N_BUCKETS):
        b = jnp.where(rel >= BUCKET_START[t], relb_ref[t, h], b)
    return b


def _top_blocks(score, idx, n_idx, axis):
    chosen = jnp.zeros(score.shape, F32)
    firsts = []
    left = score
    for _ in range(MOBA_TOPK):
        mx = jnp.max(left, axis=axis, keepdims=True)
        first = jnp.min(jnp.where(left == mx, idx, float(n_idx)), axis=axis, keepdims=True)
        hit = idx == first
        chosen = jnp.where(hit, 1.0, chosen)
        left = jnp.where(hit, -jnp.inf, left)
        firsts.append(first)
    return chosen, firsts


def _proj_kernel(x_ref, w_ref, *out_refs, epilogue):
    z = jnp.dot(x_ref[...], w_ref[...], preferred_element_type=F32)
    epilogue(z, *out_refs)


def _proj(xb, wb, col0, n_cols, tm, tn, outs, epilogue):
    m, k = xb.shape
    assert m % tm == 0 and n_cols % tn == 0 and col0 % tn == 0
    return pl.pallas_call(
        functools.partial(_proj_kernel, epilogue=epilogue),
        grid=(n_cols // tn, m // tm),
        in_specs=[pl.BlockSpec((tm, k), lambda j, i: (i, 0)),
                  pl.BlockSpec((k, tn), lambda j, i: (0, col0 // tn + j))],
        out_specs=[o[1] for o in outs],
        out_shape=[o[0] for o in outs],
        compiler_params=_params("arbitrary", "arbitrary"),
    )(xb, wb)


def _rowmajor_out(m, n_cols, tm, tn, dtype):
    return (jax.ShapeDtypeStruct((m, n_cols), dtype), pl.BlockSpec((tm, tn), lambda j, i: (i, j)))


def _transposed_out(m, n_cols, tm, tn, dtype):
    return (jax.ShapeDtypeStruct((n_cols, m), dtype), pl.BlockSpec((tn, tm), lambda j, i: (j, i)))


def _ep_f32(z, o_ref):
    o_ref[...] = z


def _ep_q_scaled(z, o_ref):
    o_ref[...] = z * (1.0 / math.sqrt(HEAD_DIM))


def _ep_qT(z, oT_ref):
    oT_ref[...] = (z * (1.0 / math.sqrt(HEAD_DIM))).T.astype(BF16)


def _ep_k(z, kT_ref, kb_ref, km_ref):
    tm, tn = z.shape
    kT_ref[...] = z.T
    zb = z.astype(BF16)
    for h in range(tn // HEAD_DIM):
        kb_ref[h] = zb[:, h * HEAD_DIM:(h + 1) * HEAD_DIM]
    km_ref[0] = jnp.mean(z.reshape(tm // MOBA_BLOCK, MOBA_BLOCK, tn), axis=1)


def _ep_v(z, vT_ref, vb_ref):
    tm, tn = z.shape
    zT = z.T
    vT_ref[...] = zT
    zTb = zT.astype(BF16)
    pad_rows = V_ROWS - HEAD_DIM
    ones_row = jnp.where(lax.broadcasted_iota(jnp.int32, (pad_rows, MOBA_BLOCK), 0) == 0,
                         1.0, 0.0).astype(BF16)
    for h in range(tn // HEAD_DIM):
        for jb in range(tm // MOBA_BLOCK):
            vb_ref[h, jb, 0:HEAD_DIM, :] = zTb[h * HEAD_DIM:(h + 1) * HEAD_DIM,
                                               jb * MOBA_BLOCK:(jb + 1) * MOBA_BLOCK]
            vb_ref[h, jb, HEAD_DIM:V_ROWS, :] = ones_row


def _ep_gate(z, o_ref):
    o_ref[...] = jax.nn.sigmoid(z).astype(BF16)


def _pool_prompt_kernel(u_ref, halo_ref, o_ref):
    i = pl.program_id(0)
    tm = u_ref.shape[0]
    u = u_ref[...]
    halo = jnp.where(i > 0, halo_ref[...], 0.0)
    ext = jnp.concatenate([halo, u], axis=0)
    pos = i * tm + lax.broadcasted_iota(jnp.int32, (tm, 1), 0)
    for g, w in enumerate(POOL_WINDOWS):
        cols = slice(g * POOL_GROUP_DIM, (g + 1) * POOL_GROUP_DIM)
        s = ext[:, cols]
        span = 1
        while span < w:
            s = s + pltpu.roll(s, span, axis=0)
            span *= 2
        cnt = jnp.minimum(pos + 1, w).astype(F32)
        o_ref[:, cols] = (s[POOL_HALO:] / cnt - u[:, cols]).astype(o_ref.dtype)


def _pool_prompt(u, tm):
    m = u.shape[0]
    hb = tm // POOL_HALO
    return pl.pallas_call(
        _pool_prompt_kernel,
        grid=(m // tm,),
        in_specs=[pl.BlockSpec((tm, D_POOL), lambda i: (i, 0)),
                  pl.BlockSpec((POOL_HALO, D_POOL), lambda i: (jnp.maximum(i * hb - 1, 0), 0))],
        out_specs=pl.BlockSpec((tm, D_POOL), lambda i: (i, 0)),
        out_shape=jax.ShapeDtypeStruct((m, D_POOL), BF16),
        compiler_params=_params("arbitrary"),
    )(u, u)


def _pool_sample_kernel(st_ref, u_ref, o_ref):
    u = u_ref[...]
    for g, w in enumerate(POOL_WINDOWS):
        cols = slice(g * POOL_GROUP_DIM, (g + 1) * POOL_GROUP_DIM)
        win = u[:, cols]
        for t in range(POOL_STATE - (w - 1), POOL_STATE):
            win = win + st_ref[t, :, cols]
        o_ref[:, cols] = (win / float(w) - u[:, cols]).astype(o_ref.dtype)


def _pool_sample(state_t, u):
    b = u.shape[0]
    return pl.pallas_call(
        _pool_sample_kernel,
        out_shape=jax.ShapeDtypeStruct((b, D_POOL), BF16),
        compiler_params=pltpu.CompilerParams(vmem_limit_bytes=VMEM_LIMIT),
    )(state_t, u)


def _mix_kernel(pooled_ref, a_ref, gp_ref, ga_ref, x_ref, wpool_ref, pscale_ref, wao_ref,
                wo_ref, g_ref, b_ref, h_ref, hb_ref, *, a_transposed):
    parts = [jnp.dot(pooled_ref[:, g * POOL_GROUP_DIM:(g + 1) * POOL_GROUP_DIM], wpool_ref[g],
                     preferred_element_type=F32) for g in range(N_POOL_GROUPS)]
    p_br = jnp.concatenate(parts, axis=-1) * pscale_ref[...]
    if a_transposed:
        a_br = lax.dot_general(a_ref[...], wao_ref[...], (((0,), (0,)), ((), ())),
                               preferred_element_type=F32)
    else:
        a_br = jnp.dot(a_ref[...], wao_ref[...], preferred_element_type=F32)
    merged = gp_ref[...].astype(F32) * p_br + ga_ref[...].astype(F32) * a_br
    mix = jnp.dot(merged.astype(BF16), wo_ref[...], preferred_element_type=F32)
    h = _layer_norm(ALPHA * x_ref[...] + mix, g_ref[...], b_ref[...])
    h_ref[...] = h
    hb_ref[...] = h.astype(BF16)


def _mix(pooled, a, gates, x, wpool, pscale, wao, wo, ln_g, ln_b, tm, a_transposed):
    m = x.shape[0]
    row = lambda i: (i, 0)
    const2 = lambda i: (0, 0)
    a_spec = (pl.BlockSpec((D_ATTN, tm), lambda i: (0, i)) if a_transposed
              else pl.BlockSpec((tm, D_ATTN), row))
    return pl.pallas_call(
        functools.partial(_mix_kernel, a_transposed=a_transposed),
        grid=(m // tm,),
        in_specs=[pl.BlockSpec((tm, D_POOL), row),
                  a_spec,
                  pl.BlockSpec((tm, D_MODEL), lambda i: (i, 0)),
                  pl.BlockSpec((tm, D_MODEL), lambda i: (i, 1)),
                  pl.BlockSpec((tm, D_MODEL), row),
                  pl.BlockSpec((N_POOL_GROUPS, POOL_GROUP_DIM, POOL_OUT_GROUP), lambda i: (0, 0, 0)),
                  pl.BlockSpec((1, D_MODEL), const2),
                  pl.BlockSpec((D_ATTN, D_MODEL), const2),
                  pl.BlockSpec((D_MODEL, D_MODEL), const2),
                  pl.BlockSpec((1, D_MODEL), const2),
                  pl.BlockSpec((1, D_MODEL), const2)],
        out_specs=[pl.BlockSpec((tm, D_MODEL), row), pl.BlockSpec((tm, D_MODEL), row)],
        out_shape=[jax.ShapeDtypeStruct((m, D_MODEL), F32), jax.ShapeDtypeStruct((m, D_MODEL), BF16)],
        compiler_params=_params("arbitrary"),
    )(pooled, a, gates, gates, x, wpool, pscale, wao, wo, ln_g, ln_b)


def _ffn_kernel(hb_ref, h_ref, wg_ref, wu_ref, wo_ref, g_ref, b_ref, o_ref, acc_ref):
    f = pl.program_id(1)

    @pl.when(f == 0)
    def _():
        acc_ref[...] = jnp.zeros_like(acc_ref)

    hb = hb_ref[...]
    gate = jnp.dot(hb, wg_ref[...], preferred_element_type=F32)
    up = jnp.dot(hb, wu_ref[...], preferred_element_type=F32)
    act = (gate * jax.nn.sigmoid(gate) * up).astype(BF16)
    acc_ref[...] += jnp.dot(act, wo_ref[...], preferred_element_type=F32)

    @pl.when(f == pl.num_programs(1) - 1)
    def _():
        o_ref[...] = _layer_norm(ALPHA * h_ref[...] + acc_ref[...], g_ref[...], b_ref[...])


def _ffn(hb, h, w_in, w_out, ln_g, ln_b, tm, tf):
    m = h.shape[0]
    nf = D_FF // tf
    row = lambda i, f: (i, 0)
    const2 = lambda i, f: (0, 0)
    return pl.pallas_call(
        _ffn_kernel,
        grid=(m // tm, nf),
        in_specs=[pl.BlockSpec((tm, D_MODEL), row),
                  pl.BlockSpec((tm, D_MODEL), row),
                  pl.BlockSpec((D_MODEL, tf), lambda i, f: (0, f)),
                  pl.BlockSpec((D_MODEL, tf), lambda i, f: (0, nf + f)),
                  pl.BlockSpec((tf, D_MODEL), lambda i, f: (f, 0)),
                  pl.BlockSpec((1, D_MODEL), const2),
                  pl.BlockSpec((1, D_MODEL), const2)],
        out_specs=pl.BlockSpec((tm, D_MODEL), row),
        out_shape=jax.ShapeDtypeStruct((m, D_MODEL), F32),
        scratch_shapes=[pltpu.VMEM((tm, D_MODEL), F32)],
        compiler_params=_params("arbitrary", "arbitrary"),
    )(hb, h, w_in, w_in, w_out, ln_g, ln_b)


def _attn_kernel(relb_ref, qT_ref, k_ref, vT_ref, km_ref, oT_ref,
                 bown_ref, bprev_ref, sel_ref, selfar_ref, s_ref):
    c = pl.program_id(1)
    ng, nb = k_ref.shape[:2]
    heads = [pl.program_id(0) * ng + g for g in range(ng)]
    blk = MOBA_BLOCK
    slot_prev, slot_own = nb, nb + 1

    @pl.when(c == 0)
    def _():
        kk = lax.broadcasted_iota(jnp.int32, (blk, blk), 0)
        qq = lax.broadcasted_iota(jnp.int32, (blk, blk), 1)
        rel = qq - kk
        for g, h in enumerate(heads):
            bown_ref[g] = jnp.where(rel >= 0, _bias_tile(relb_ref, h, rel), NEG_BIG)
            bprev_ref[g] = _bias_tile(relb_ref, h, rel + blk)

    qTs = [qT_ref[g] for g in range(ng)]

    jidx = lax.broadcasted_iota(jnp.int32, (nb, blk), 0).astype(F32)
    cf = c.astype(F32)
    for g in range(ng):
        km = km_ref[g]
        km_hi = km.astype(BF16)
        r1 = km - km_hi.astype(F32)
        km_mid = r1.astype(BF16)
        km_lo = (r1 - km_mid.astype(F32)).astype(BF16)
        score = (jnp.dot(km_hi, qTs[g], preferred_element_type=F32)
                 + jnp.dot(km_mid, qTs[g], preferred_element_type=F32)
                 + jnp.dot(km_lo, qTs[g], preferred_element_type=F32))
        chosen, _ = _top_blocks(jnp.where(jidx < cf, score, -jnp.inf), jidx, nb, 0)
        sel_ref[g] = jnp.where(jidx < cf, chosen, 0.0)
        selfar_ref[g] = jnp.where(jidx < cf - 1.0, chosen, 0.0)

    n_far = jnp.maximum(c - 1, 0)
    trips = (n_far + ATTN_UNROLL - 1) // ATTN_UNROLL
    far_bias = [relb_ref[N_BUCKETS - 1, h] * LOG2E for h in heads]
    jp = jnp.maximum(c - 1, 0)
    keep_prev = [sel_ref[g, pl.ds(jp, 1), :] > 0.5 for g in range(ng)]

    def col_max(s):
        return jnp.max(s, axis=0, keepdims=True)

    ms = []
    for g in range(ng):
        s_own = (jnp.dot(k_ref[g, c], qTs[g], preferred_element_type=F32) + bown_ref[g]) * LOG2E
        s_ref[g, slot_own] = s_own
        s_prev = (jnp.dot(k_ref[g, jp], qTs[g], preferred_element_type=F32) + bprev_ref[g]) * LOG2E
        s_ref[g, slot_prev] = s_prev
        ms.append(jnp.maximum(col_max(s_own), jnp.where(keep_prev[g], col_max(s_prev), NEG_BIG)))

    def pass_a(t, ms):
        ms = list(ms)
        for u in range(ATTN_UNROLL):
            j = t * ATTN_UNROLL + u
            for g in range(ng):
                s = jnp.dot(k_ref[g, j], qTs[g], preferred_element_type=F32) * LOG2E
                s_ref[g, j] = s
                keep = selfar_ref[g, pl.ds(j, 1), :] > 0.5
                ms[g] = jnp.maximum(ms[g], jnp.where(keep, col_max(s) + far_bias[g], NEG_BIG))
        return tuple(ms)

    ms = lax.fori_loop(0, trips, pass_a, tuple(ms))

    accs = []
    for g in range(ng):
        p = jnp.exp2(s_ref[g, slot_own] - ms[g])
        acc = jnp.dot(vT_ref[g, c], p.astype(BF16), preferred_element_type=F32)
        p = jnp.exp2(s_ref[g, slot_prev] - jnp.where(keep_prev[g], ms[g], POS_BIG))
        accs.append(acc + jnp.dot(vT_ref[g, jp], p.astype(BF16), preferred_element_type=F32))
    m_far = [ms[g] - far_bias[g] for g in range(ng)]

    def pass_b(t, accs):
        accs = list(accs)
        for u in range(ATTN_UNROLL):
            j = t * ATTN_UNROLL + u
            for g in range(ng):
                keep = selfar_ref[g, pl.ds(j, 1), :] > 0.5
                p = jnp.exp2(s_ref[g, j] - jnp.where(keep, m_far[g], POS_BIG))
                accs[g] = accs[g] + jnp.dot(vT_ref[g, j], p.astype(BF16), preferred_element_type=F32)
        return tuple(accs)

    accs = lax.fori_loop(0, trips, pass_b, tuple(accs))
    for g in range(ng):
        oT_ref[g] = (accs[g][:HEAD_DIM] / accs[g][HEAD_DIM:HEAD_DIM + 1]).astype(oT_ref.dtype)


def _attn_prompt(rel_bias, qT, k4, vT4, km3):
    nh, _, n = qT.shape
    nb = n // MOBA_BLOCK
    ng = ATTN_HEADS
    assert nb % ATTN_UNROLL == 0 and nh % ng == 0
    tile = (ng, MOBA_BLOCK, MOBA_BLOCK)
    return pl.pallas_call(
        _attn_kernel,
        grid=(nh // ng, nb),
        in_specs=[pl.BlockSpec(memory_space=pltpu.SMEM),
                  pl.BlockSpec((ng, HEAD_DIM, MOBA_BLOCK), lambda h, c: (h, 0, c)),
                  pl.BlockSpec((ng, nb, MOBA_BLOCK, HEAD_DIM), lambda h, c: (h, 0, 0, 0)),
                  pl.BlockSpec((ng, nb, V_ROWS, MOBA_BLOCK), lambda h, c: (h, 0, 0, 0)),
                  pl.BlockSpec((ng, nb, HEAD_DIM), lambda h, c: (h, 0, 0))],
        out_specs=pl.BlockSpec((ng, HEAD_DIM, MOBA_BLOCK), lambda h, c: (h, 0, c)),
        out_shape=jax.ShapeDtypeStruct((nh, HEAD_DIM, n), BF16),
        scratch_shapes=[pltpu.VMEM(tile, F32),
                        pltpu.VMEM(tile, F32),
                        pltpu.VMEM((ng, nb, MOBA_BLOCK), F32),
                        pltpu.VMEM((ng, nb, MOBA_BLOCK), F32),
                        pltpu.VMEM((ng, nb + 2, MOBA_BLOCK, MOBA_BLOCK), F32)],
        compiler_params=_params("arbitrary", "arbitrary"),
    )(rel_bias, qT, k4, vT4, km3)


def _cache_scan_kernel(pt_ref, *refs):
    n_pages = 2 * SCAN_BLOCKS
    page_refs = refs[:n_pages]
    q_ref, l_ref, sel_ref, sc_ref = refs[n_pages:]
    t = pl.program_id(1)
    nt = pl.num_programs(1)
    q = q_ref[0]
    lane = lax.broadcasted_iota(jnp.int32, sc_ref.shape, 1)

    @pl.when(t == 0)
    def _():
        sc_ref[...] = jnp.zeros_like(sc_ref)

    sc = sc_ref[...]
    for jj in range(SCAN_BLOCKS):
        halves = [jnp.sum(page_refs[2 * jj + i][0] * q, axis=1) for i in range(2)]
        logit = jnp.concatenate(halves, axis=-1)
        l_ref[0, :, jj, :] = logit
        mean = jnp.sum(logit, axis=-1, keepdims=True) * (1.0 / MOBA_BLOCK)
        sc = jnp.where(lane == t * SCAN_BLOCKS + jj, mean, sc)
    sc_ref[...] = sc

    @pl.when(t == nt - 1)
    def _():
        nb = nt * SCAN_BLOCKS
        lane_f = lane.astype(F32)
        _, firsts = _top_blocks(jnp.where(lane < nb, sc, -jnp.inf), lane_f, LANES, 1)
        out = jnp.zeros(sc.shape, F32)
        for r, first in enumerate(firsts):
            out = jnp.where(lane == r, first, out)
        sel_ref[0] = out.astype(jnp.int32)


def _cache_scan(page_table, cache_t, q_rep):
    b, n_pages = page_table.shape
    page = cache_t.shape[-1]
    assert MOBA_BLOCK == 2 * page and page == LANES
    nb = n_pages // 2
    assert nb % SCAN_BLOCKS == 0 and MOBA_TOPK <= nb <= LANES
    pps = 2 * SCAN_BLOCKS
    page_specs = [pl.BlockSpec((1, N_HEADS, HEAD_DIM, page),
                               functools.partial(lambda s, t, pt, i: (pt[s, t * pps + i], 0, 0, 0), i=i))
                  for i in range(pps)]
    return pl.pallas_call(
        _cache_scan_kernel,
        grid_spec=pltpu.PrefetchScalarGridSpec(
            num_scalar_prefetch=1,
            grid=(b, nb // SCAN_BLOCKS),
            in_specs=page_specs + [pl.BlockSpec((1, N_HEADS, HEAD_DIM, page),
                                                lambda s, t, pt: (s, 0, 0, 0))],
            out_specs=[pl.BlockSpec((1, N_HEADS, SCAN_BLOCKS, MOBA_BLOCK),
                                    lambda s, t, pt: (s, 0, t, 0)),
                       pl.BlockSpec((1, N_HEADS, LANES), lambda s, t, pt: (s, 0, 0))],
            scratch_shapes=[pltpu.VMEM((N_HEADS, LANES), F32)]),
        out_shape=[jax.ShapeDtypeStruct((b, N_HEADS, nb, MOBA_BLOCK), F32),
                   jax.ShapeDtypeStruct((b, N_HEADS, LANES), jnp.int32)],
        compiler_params=_params("arbitrary", "arbitrary"),
    )(page_table, *([cache_t] * pps), q_rep)


def _decode_attend_kernel(sel_ref, pt_ref, relb_ref, l_ref, *refs):
    v_refs = refs[:2 * MOBA_TOPK]
    q_ref, kn_ref, vn_ref, a_ref = refs[2 * MOBA_TOPK:]
    s = pl.program_id(0)
    h = pl.program_id(1)
    nb = l_ref.shape[2]
    page = v_refs[0].shape[-1]
    pos = lax.broadcasted_iota(jnp.int32, (1, MOBA_BLOCK), 1)
    far_bias = relb_ref[N_BUCKETS - 1, h]
    last_bias = _bias_tile(relb_ref, h, MOBA_BLOCK - pos)
    s_own = jnp.sum(q_ref[0, 0] * kn_ref[0, 0], axis=-1, keepdims=True) + relb_ref[0, h]
    logits = []
    m = s_own
    for r in range(MOBA_TOPK):
        blk = sel_ref[s, h * MOBA_TOPK + r]
        row = l_ref[0, 0, pl.ds(blk, 1), :]
        row = row + jnp.where(blk == nb - 1, last_bias, far_bias)
        logits.append(row)
        m = jnp.maximum(m, jnp.max(row, axis=-1, keepdims=True))
    w_own = jnp.exp(s_own - m)
    num = w_own * vn_ref[0, 0]
    den = w_own
    for r in range(MOBA_TOPK):
        p = jnp.exp(logits[r] - m)
        den = den + jnp.sum(p, axis=-1, keepdims=True)
        pb = jnp.broadcast_to(p, (8, MOBA_BLOCK)).astype(BF16)
        for i in range(2):
            vt = v_refs[2 * r + i][0, 0].astype(BF16)
            pv = lax.dot_general(pb[:, i * page:(i + 1) * page], vt, (((1,), (1,)), ((), ())),
                                 preferred_element_type=F32)
            num = num + pv[0:1]
    a_ref[0, 0] = num / den


def _decode_attend(sel2, page_table, rel_bias, logits, cache_vt, q4, kn4, vn4):
    b, nh, nb, _ = logits.shape
    page = cache_vt.shape[-1]
    tok = pl.BlockSpec((1, 1, 1, HEAD_DIM), lambda s, h, sel, pt: (s, h, 0, 0))
    v_specs = [pl.BlockSpec((1, 1, HEAD_DIM, page),
                            functools.partial(
                                lambda s, h, sel, pt, r, i: (pt[s, 2 * sel[s, h * MOBA_TOPK + r] + i], h, 0, 0),
                                r=r, i=i))
               for r in range(MOBA_TOPK) for i in range(2)]
    return pl.pallas_call(
        _decode_attend_kernel,
        grid_spec=pltpu.PrefetchScalarGridSpec(
            num_scalar_prefetch=2,
            grid=(b, nh),
            in_specs=[pl.BlockSpec(memory_space=pltpu.SMEM),
                      pl.BlockSpec((1, 1, nb, MOBA_BLOCK), lambda s, h, sel, pt: (s, h, 0, 0))]
                     + v_specs + [tok, tok, tok],
            out_specs=tok),
        out_shape=jax.ShapeDtypeStruct((b, nh, 1, HEAD_DIM), F32),
        compiler_params=_params("arbitrary", "arbitrary"),
    )(sel2, page_table, rel_bias, logits, *([cache_vt] * (2 * MOBA_TOPK)), q4, kn4, vn4)


def kernel(x_prompt, x_sample, cache_k, cache_v, state_pool, page_table, rel_bias, w_in, w_pool,
           pool_scale, w_attn_out, w_o, ln1_g, ln1_b, w_ffn_in, w_ffn_out, ln2_g, ln2_b):
    assert w_in.shape[0] == 1 and x_prompt.shape[0] == 1 and x_sample.shape[1] == 1
    n = x_prompt.shape[1]
    bs = x_sample.shape[0]
    nb = n // MOBA_BLOCK
    page = cache_k.shape[2]
    past_len = page_table.shape[1] * page
    assert past_len % MOBA_BLOCK == 0 and past_len >= max(POOL_WINDOWS)

    w_in_b = w_in[0].astype(BF16)
    w_pool_b = w_pool[0].astype(BF16)
    w_ao_b = w_attn_out[0].astype(BF16)
    w_o_b = w_o[0].astype(BF16)
    w_fi_b = w_ffn_in[0].astype(BF16)
    w_fo_b = w_ffn_out[0].astype(BF16)
    pscale = pool_scale[0][None]
    g1, b1, g2, b2 = ln1_g[0][None], ln1_b[0][None], ln2_g[0][None], ln2_b[0][None]

    xp = x_prompt[0]
    xs = x_sample[:, 0]
    xp_b = xp.astype(BF16)
    xs_b = xs.astype(BF16)
    col_q, col_k, col_v, col_g = D_POOL, D_POOL + D_ATTN, D_POOL + 2 * D_ATTN, D_POOL + 3 * D_ATTN
    hd = (N_HEADS, HEAD_DIM)

    tm, tn = 1024, 1024
    (u_p,) = _proj(xp_b, w_in_b, 0, D_POOL, tm, tn, [_rowmajor_out(n, D_POOL, tm, tn, F32)], _ep_f32)
    (qT_p,) = _proj(xp_b, w_in_b, col_q, D_ATTN, tm, tn,
                    [_transposed_out(n, D_ATTN, tm, tn, BF16)], _ep_qT)
    kT_p, kb_p, km_p = _proj(
        xp_b, w_in_b, col_k, D_ATTN, tm, tn,
        [_transposed_out(n, D_ATTN, tm, tn, F32),
         (jax.ShapeDtypeStruct((N_HEADS, n, HEAD_DIM), BF16),
          pl.BlockSpec((N_HEADS, tm, HEAD_DIM), lambda j, i: (j, i, 0))),
         (jax.ShapeDtypeStruct((n // tm, tm // MOBA_BLOCK, D_ATTN), F32),
          pl.BlockSpec((1, tm // MOBA_BLOCK, tn), lambda j, i: (i, 0, j)))],
        _ep_k)
    vT_p, vb_p = _proj(
        xp_b, w_in_b, col_v, D_ATTN, tm, tn,
        [_transposed_out(n, D_ATTN, tm, tn, F32),
         (jax.ShapeDtypeStruct((N_HEADS, nb, V_ROWS, MOBA_BLOCK), BF16),
          pl.BlockSpec((N_HEADS, tm // MOBA_BLOCK, V_ROWS, MOBA_BLOCK), lambda j, i: (j, i, 0, 0)))],
        _ep_v)
    (gates_p,) = _proj(xp_b, w_in_b, col_g, 2 * D_MODEL, tm, tn,
                       [_rowmajor_out(n, 2 * D_MODEL, tm, tn, BF16)], _ep_gate)

    km3 = km_p.reshape(nb, N_HEADS, HEAD_DIM).transpose(1, 0, 2)
    aT = _attn_prompt(rel_bias, qT_p.reshape(N_HEADS, HEAD_DIM, n),
                      kb_p.reshape(N_HEADS, nb, MOBA_BLOCK, HEAD_DIM), vb_p, km3)

    pooled_p = _pool_prompt(u_p, 512)
    h1_p, h1b_p = _mix(pooled_p, aT.reshape(D_ATTN, n), gates_p, xp, w_pool_b, pscale, w_ao_b, w_o_b,
                       g1, b1, 256, True)
    y_p = _ffn(h1b_p, h1_p, w_fi_b, w_fo_b, g2, b2, 512, 512)

    row32 = lambda nc, dt: [_rowmajor_out(bs, nc, bs, tn, dt)]
    (u_s,) = _proj(xs_b, w_in_b, 0, D_POOL, bs, tn, row32(D_POOL, F32), _ep_f32)
    (q_s,) = _proj(xs_b, w_in_b, col_q, D_ATTN, bs, tn, row32(D_ATTN, F32), _ep_q_scaled)
    (k_s,) = _proj(xs_b, w_in_b, col_k, D_ATTN, bs, tn, row32(D_ATTN, F32), _ep_f32)
    (v_s,) = _proj(xs_b, w_in_b, col_v, D_ATTN, bs, tn, row32(D_ATTN, F32), _ep_f32)
    (gates_s,) = _proj(xs_b, w_in_b, col_g, 2 * D_MODEL, bs, tn, row32(2 * D_MODEL, BF16), _ep_gate)

    ck_t = cache_k[0].transpose(0, 2, 3, 1)
    cv_t = cache_v[0].transpose(0, 2, 3, 1)
    q_rep = jnp.broadcast_to(q_s.reshape(bs, *hd, 1), (bs, *hd, page))
    logits_s, sel_s = _cache_scan(page_table, ck_t, q_rep)
    sel2 = sel_s[:, :, :MOBA_TOPK].reshape(bs, N_HEADS * MOBA_TOPK)
    a_s = _decode_attend(sel2, page_table, rel_bias, logits_s, cv_t,
                         q_s.reshape(bs, N_HEADS, 1, HEAD_DIM), k_s.reshape(bs, N_HEADS, 1, HEAD_DIM),
                         v_s.reshape(bs, N_HEADS, 1, HEAD_DIM))
    a_s = a_s.reshape(bs, D_ATTN).astype(BF16)

    state_t = state_pool[0].transpose(1, 0, 2)
    pooled_s = _pool_sample(state_t, u_s)
    h1_s, h1b_s = _mix(pooled_s, a_s, gates_s, xs, w_pool_b, pscale, w_ao_b, w_o_b, g1, b1, bs, False)
    y_s = _ffn(h1b_s, h1_s, w_fi_b, w_fo_b, g2, b2, bs, 512)

    new_pool_p = u_p[n - POOL_STATE:][None, None]
    new_pool_s = jnp.concatenate([state_t[1:], u_s[None]], axis=0).transpose(1, 0, 2)[None]
    k_out = kT_p.reshape(*hd, n).transpose(2, 0, 1)[None, None]
    v_out = vT_p.reshape(*hd, n).transpose(2, 0, 1)[None, None]
    return (y_p[None], y_s[:, None], k_out, v_out,
            k_s.reshape(1, bs, 1, *hd), v_s.reshape(1, bs, 1, *hd),
            new_pool_p, new_pool_s)
```

```python
import functools
import math

import numpy as np
import jax
import jax.numpy as jnp
from jax import lax
from jax.experimental import pallas as pl
from jax.experimental.pallas import tpu as pltpu

D_MODEL = 2048
N_HEADS = 16
HEAD_DIM = 64
D_ATTN = N_HEADS * HEAD_DIM
MOBA_BLOCK = 256
MOBA_TOPK = 3
POOL_WINDOWS = (2, 4, 8, 16)
N_POOL_GROUPS = len(POOL_WINDOWS)
POOL_GROUP_DIM = 256
D_POOL = N_POOL_GROUPS * POOL_GROUP_DIM
POOL_OUT_GROUP = D_MODEL // N_POOL_GROUPS
POOL_STATE = max(POOL_WINDOWS) - 1
POOL_HALO = 16
N_BUCKETS = 32
MAX_DISTANCE = 128
D_FF = -(-8 * D_MODEL // (3 * 256)) * 256
ALPHA = 2.0 ** 0.25
LN_EPS = 1e-5
NEG_BIG = -1e30
POS_BIG = 1e30
LANES = 128
V_ROWS = 80
ATTN_UNROLL = 4
ATTN_HEADS = 2
LOG2E = 1.4426950408889634
SCAN_BLOCKS = 4
VMEM_LIMIT = 56 * 1024 * 1024

F32 = jnp.float32
BF16 = jnp.bfloat16


def _t5_bucket_np(rel):
    n = np.maximum(rel, 0)
    max_exact = N_BUCKETS // 2
    nf = np.maximum(n, 1).astype(np.float32)
    large = max_exact + (np.log(nf / max_exact) / math.log(MAX_DISTANCE / max_exact)
                         * (N_BUCKETS - max_exact)).astype(np.int32)
    large = np.minimum(large, N_BUCKETS - 1)
    return np.where(n < max_exact, n, large)


_BUCKETS = _t5_bucket_np(np.arange(0, 4 * MOBA_BLOCK))
BUCKET_START = tuple(int(np.argmax(_BUCKETS >= t)) for t in range(N_BUCKETS))
FAR_REL = BUCKET_START[N_BUCKETS - 1]
assert FAR_REL <= MOBA_BLOCK


def _params(*sem):
    return pltpu.CompilerParams(dimension_semantics=sem, vmem_limit_bytes=VMEM_LIMIT)


def _layer_norm(y, g, b):
    mu = jnp.mean(y, axis=-1, keepdims=True)
    d = y - mu
    var = jnp.mean(d * d, axis=-1, keepdims=True)
    return d * lax.rsqrt(var + LN_EPS) * g + b


def _bias_tile(relb_ref, h, rel):
    b = jnp.full(rel.shape, relb_ref[0, h], F32)
    for t in range(1, N_BUCKETS):
        b = jnp.where(rel >= BUCKET_START[t], relb_ref[t, h], b)
    return b


def _top_blocks(score, idx, n_idx, axis):
    chosen = jnp.zeros(score.shape, F32)
    firsts = []
    left = score
    for _ in range(MOBA_TOPK):
        mx = jnp.max(left, axis=axis, keepdims=True)
        first = jnp.min(jnp.where(left == mx, idx, float(n_idx)), axis=axis, keepdims=True)
        hit = idx == first
        chosen = jnp.where(hit, 1.0, chosen)
        left = jnp.where(hit, -jnp.inf, left)
        firsts.append(first)
    return chosen, firsts


def _proj_kernel(x_ref, w_ref, *out_refs, epilogue):
    z = jnp.dot(x_ref[...], w_ref[...], preferred_element_type=F32)
    epilogue(z, *out_refs)


def _proj(xb, wb, col0, n_cols, tm, tn, outs, epilogue):
    m, k = xb.shape
    assert m % tm == 0 and n_cols % tn == 0 and col0 % tn == 0
    return pl.pallas_call(
        functools.partial(_proj_kernel, epilogue=epilogue),
        grid=(n_cols // tn, m // tm),
        in_specs=[pl.BlockSpec((tm, k), lambda j, i: (i, 0)),
                  pl.BlockSpec((k, tn), lambda j, i: (0, col0 // tn + j))],
        out_specs=[o[1] for o in outs],
        out_shape=[o[0] for o in outs],
        compiler_params=_params("arbitrary", "arbitrary"),
    )(xb, wb)


def _rowmajor_out(m, n_cols, tm, tn, dtype):
    return (jax.ShapeDtypeStruct((m, n_cols), dtype), pl.BlockSpec((tm, tn), lambda j, i: (i, j)))


def _transposed_out(m, n_cols, tm, tn, dtype):
    return (jax.ShapeDtypeStruct((n_cols, m), dtype), pl.BlockSpec((tn, tm), lambda j, i: (j, i)))


def _ep_f32(z, o_ref):
    o_ref[...] = z


def _ep_q_scaled(z, o_ref):
    o_ref[...] = z * (1.0 / math.sqrt(HEAD_DIM))


def _ep_qT(z, oT_ref):
    oT_ref[...] = (z * (1.0 / math.sqrt(HEAD_DIM))).T.astype(BF16)


def _ep_k(z, kT_ref, kb_ref, km_ref):
    tm, tn = z.shape
    kT_ref[...] = z.T
    zb = z.astype(BF16)
    for h in range(tn // HEAD_DIM):
        kb_ref[h] = zb[:, h * HEAD_DIM:(h + 1) * HEAD_DIM]
    km_ref[0] = jnp.mean(z.reshape(tm // MOBA_BLOCK, MOBA_BLOCK, tn), axis=1)


def _ep_v(z, vT_ref, vb_ref):
    tm, tn = z.shape
    zT = z.T
    vT_ref[...] = zT
    zTb = zT.astype(BF16)
    pad_rows = V_ROWS - HEAD_DIM
    ones_row = jnp.where(lax.broadcasted_iota(jnp.int32, (pad_rows, MOBA_BLOCK), 0) == 0,
                         1.0, 0.0).astype(BF16)
    for h in range(tn // HEAD_DIM):
        for jb in range(tm // MOBA_BLOCK):
            vb_ref[h, jb, 0:HEAD_DIM, :] = zTb[h * HEAD_DIM:(h + 1) * HEAD_DIM,
                                               jb * MOBA_BLOCK:(jb + 1) * MOBA_BLOCK]
            vb_ref[h, jb, HEAD_DIM:V_ROWS, :] = ones_row


def _ep_gate(z, o_ref):
    o_ref[...] = jax.nn.sigmoid(z).astype(BF16)


def _pool_prompt_kernel(u_ref, halo_ref, o_ref):
    i = pl.program_id(0)
    tm = u_ref.shape[0]
    u = u_ref[...]
    halo = jnp.where(i > 0, halo_ref[...], 0.0)
    ext = jnp.concatenate([halo, u], axis=0)
    pos = i * tm + lax.broadcasted_iota(jnp.int32, (tm, 1), 0)
    for g, w in enumerate(POOL_WINDOWS):
        cols = slice(g * POOL_GROUP_DIM, (g + 1) * POOL_GROUP_DIM)
        s = ext[:, cols]
        span = 1
        while span < w:
            s = s + pltpu.roll(s, span, axis=0)
            span *= 2
        cnt = jnp.minimum(pos + 1, w).astype(F32)
        o_ref[:, cols] = (s[POOL_HALO:] / cnt - u[:, cols]).astype(o_ref.dtype)


def _pool_prompt(u, tm):
    m = u.shape[0]
    hb = tm // POOL_HALO
    return pl.pallas_call(
        _pool_prompt_kernel,
        grid=(m // tm,),
        in_specs=[pl.BlockSpec((tm, D_POOL), lambda i: (i, 0)),
                  pl.BlockSpec((POOL_HALO, D_POOL), lambda i: (jnp.maximum(i * hb - 1, 0), 0))],
        out_specs=pl.BlockSpec((tm, D_POOL), lambda i: (i, 0)),
        out_shape=jax.ShapeDtypeStruct((m, D_POOL), BF16),
        compiler_params=_params("arbitrary"),
    )(u, u)


def _pool_sample_kernel(st_ref, u_ref, o_ref):
    u = u_ref[...]
    for g, w in enumerate(POOL_WINDOWS):
        cols = slice(g * POOL_GROUP_DIM, (g + 1) * POOL_GROUP_DIM)
        win = u[:, cols]
        for t in range(POOL_STATE - (w - 1), POOL_STATE):
            win = win + st_ref[t, :, cols]
        o_ref[:, cols] = (win / float(w) - u[:, cols]).astype(o_ref.dtype)


def _pool_sample(state_t, u):
    b = u.shape[0]
    return pl.pallas_call(
        _pool_sample_kernel,
        out_shape=jax.ShapeDtypeStruct((b, D_POOL), BF16),
        compiler_params=pltpu.CompilerParams(vmem_limit_bytes=VMEM_LIMIT),
    )(state_t, u)


def _mix_kernel(pooled_ref, a_ref, gp_ref, ga_ref, x_ref, wpool_ref, pscale_ref, wao_ref,
                wo_ref, g_ref, b_ref, h_ref, hb_ref, *, a_transposed):
    parts = [jnp.dot(pooled_ref[:, g * POOL_GROUP_DIM:(g + 1) * POOL_GROUP_DIM], wpool_ref[g],
                     preferred_element_type=F32) for g in range(N_POOL_GROUPS)]
    p_br = jnp.concatenate(parts, axis=-1) * pscale_ref[...]
    if a_transposed:
        a_br = lax.dot_general(a_ref[...], wao_ref[...], (((0,), (0,)), ((), ())),
                               preferred_element_type=F32)
    else:
        a_br = jnp.dot(a_ref[...], wao_ref[...], preferred_element_type=F32)
    merged = gp_ref[...].astype(F32) * p_br + ga_ref[...].astype(F32) * a_br
    mix = jnp.dot(merged.astype(BF16), wo_ref[...], preferred_element_type=F32)
    h = _layer_norm(ALPHA * x_ref[...] + mix, g_ref[...], b_ref[...])
    h_ref[...] = h
    hb_ref[...] = h.astype(BF16)


def _mix(pooled, a, gates, x, wpool, pscale, wao, wo, ln_g, ln_b, tm, a_transposed):
    m = x.shape[0]
    row = lambda i: (i, 0)
    const2 = lambda i: (0, 0)
    a_spec = (pl.BlockSpec((D_ATTN, tm), lambda i: (0, i)) if a_transposed
              else pl.BlockSpec((tm, D_ATTN), row))
    return pl.pallas_call(
        functools.partial(_mix_kernel, a_transposed=a_transposed),
        grid=(m // tm,),
        in_specs=[pl.BlockSpec((tm, D_POOL), row),
                  a_spec,
                  pl.BlockSpec((tm, D_MODEL), lambda i: (i, 0)),
                  pl.BlockSpec((tm, D_MODEL), lambda i: (i, 1)),
                  pl.BlockSpec((tm, D_MODEL), row),
                  pl.BlockSpec((N_POOL_GROUPS, POOL_GROUP_DIM, POOL_OUT_GROUP), lambda i: (0, 0, 0)),
                  pl.BlockSpec((1, D_MODEL), const2),
                  pl.BlockSpec((D_ATTN, D_MODEL), const2),
                  pl.BlockSpec((D_MODEL, D_MODEL), const2),
                  pl.BlockSpec((1, D_MODEL), const2),
                  pl.BlockSpec((1, D_MODEL), const2)],
        out_specs=[pl.BlockSpec((tm, D_MODEL), row), pl.BlockSpec((tm, D_MODEL), row)],
        out_shape=[jax.ShapeDtypeStruct((m, D_MODEL), F32), jax.ShapeDtypeStruct((m, D_MODEL), BF16)],
        compiler_params=_params("arbitrary"),
    )(pooled, a, gates, gates, x, wpool, pscale, wao, wo, ln_g, ln_b)


def _ffn_kernel(hb_ref, h_ref, wg_ref, wu_ref, wo_ref, g_ref, b_ref, o_ref, acc_ref):
    f = pl.program_id(1)

    @pl.when(f == 0)
    def _():
        acc_ref[...] = jnp.zeros_like(acc_ref)

    hb = hb_ref[...]
    gate = jnp.dot(hb, wg_ref[...], preferred_element_type=F32)
    up = jnp.dot(hb, wu_ref[...], preferred_element_type=F32)
    act = (gate * jax.nn.sigmoid(gate) * up).astype(BF16)
    acc_ref[...] += jnp.dot(act, wo_ref[...], preferred_element_type=F32)

    @pl.when(f == pl.num_programs(1) - 1)
    def _():
        o_ref[...] = _layer_norm(ALPHA * h_ref[...] + acc_ref[...], g_ref[...], b_ref[...])


def _ffn(hb, h, w_in, w_out, ln_g, ln_b, tm, tf):
    m = h.shape[0]
    nf = D_FF // tf
    row = lambda i, f: (i, 0)
    const2 = lambda i, f: (0, 0)
    return pl.pallas_call(
        _ffn_kernel,
        grid=(m // tm, nf),
        in_specs=[pl.BlockSpec((tm, D_MODEL), row),
                  pl.BlockSpec((tm, D_MODEL), row),
                  pl.BlockSpec((D_MODEL, tf), lambda i, f: (0, f)),
                  pl.BlockSpec((D_MODEL, tf), lambda i, f: (0, nf + f)),
                  pl.BlockSpec((tf, D_MODEL), lambda i, f: (f, 0)),
                  pl.BlockSpec((1, D_MODEL), const2),
                  pl.BlockSpec((1, D_MODEL), const2)],
        out_specs=pl.BlockSpec((tm, D_MODEL), row),
        out_shape=jax.ShapeDtypeStruct((m, D_MODEL), F32),
        scratch_shapes=[pltpu.VMEM((tm, D_MODEL), F32)],
        compiler_params=_params("arbitrary", "arbitrary"),
    )(hb, h, w_in, w_in, w_out, ln_g, ln_b)


def _attn_kernel(pt_ref, relb_ref, qT_ref, k_ref, vT_ref, km_ref, *refs):
    n_pages = 2 * SCAN_BLOCKS
    page_refs = refs[:n_pages]
    (qrep_ref, oT_ref, l_ref, selo_ref,
     bown_ref, bprev_ref, sel_ref, selfar_ref, s_ref, sc_ref) = refs[n_pages:]
    scan_steps = pt_ref.shape[1] // n_pages
    step = pl.program_id(0) * pl.num_programs(1) + pl.program_id(1)
    _cache_scan_step(step % scan_steps, scan_steps, page_refs, qrep_ref, l_ref, selo_ref, sc_ref)

    c = pl.program_id(1)
    ng, nb = k_ref.shape[:2]
    heads = [pl.program_id(0) * ng + g for g in range(ng)]
    blk = MOBA_BLOCK
    slot_prev, slot_own = nb, nb + 1

    @pl.when(c == 0)
    def _():
        kk = lax.broadcasted_iota(jnp.int32, (blk, blk), 0)
        qq = lax.broadcasted_iota(jnp.int32, (blk, blk), 1)
        rel = qq - kk
        for g, h in enumerate(heads):
            bown_ref[g] = jnp.where(rel >= 0, _bias_tile(relb_ref, h, rel), NEG_BIG)
            bprev_ref[g] = _bias_tile(relb_ref, h, rel + blk)

    qTs = [qT_ref[g] for g in range(ng)]

    jidx = lax.broadcasted_iota(jnp.int32, (nb, blk), 0).astype(F32)
    cf = c.astype(F32)
    for g in range(ng):
        km = km_ref[g]
        km_hi = km.astype(BF16)
        r1 = km - km_hi.astype(F32)
        km_mid = r1.astype(BF16)
        km_lo = (r1 - km_mid.astype(F32)).astype(BF16)
        score = (jnp.dot(km_hi, qTs[g], preferred_element_type=F32)
                 + jnp.dot(km_mid, qTs[g], preferred_element_type=F32)
                 + jnp.dot(km_lo, qTs[g], preferred_element_type=F32))
        chosen, _ = _top_blocks(jnp.where(jidx < cf, score, -jnp.inf), jidx, nb, 0)
        sel_ref[g] = jnp.where(jidx < cf, chosen, 0.0)
        selfar_ref[g] = jnp.where(jidx < cf - 1.0, chosen, 0.0)

    n_far = jnp.maximum(c - 1, 0)
    trips = (n_far + ATTN_UNROLL - 1) // ATTN_UNROLL
    far_bias = [relb_ref[N_BUCKETS - 1, h] * LOG2E for h in heads]
    jp = jnp.maximum(c - 1, 0)
    keep_prev = [sel_ref[g, pl.ds(jp, 1), :] > 0.5 for g in range(ng)]

    def col_max(s):
        return jnp.max(s, axis=0, keepdims=True)

    ms = []
    for g in range(ng):
        s_own = (jnp.dot(k_ref[g, c], qTs[g], preferred_element_type=F32) + bown_ref[g]) * LOG2E
        s_ref[g, slot_own] = s_own
        s_prev = (jnp.dot(k_ref[g, jp], qTs[g], preferred_element_type=F32) + bprev_ref[g]) * LOG2E
        s_ref[g, slot_prev] = s_prev
        ms.append(jnp.maximum(col_max(s_own), jnp.where(keep_prev[g], col_max(s_prev), NEG_BIG)))

    def pass_a(t, ms):
        ms = list(ms)
        for u in range(ATTN_UNROLL):
            j = t * ATTN_UNROLL + u
            for g in range(ng):
                s = jnp.dot(k_ref[g, j], qTs[g], preferred_element_type=F32) * LOG2E
                s_ref[g, j] = s
                keep = selfar_ref[g, pl.ds(j, 1), :] > 0.5
                ms[g] = jnp.maximum(ms[g], jnp.where(keep, col_max(s) + far_bias[g], NEG_BIG))
        return tuple(ms)

    ms = lax.fori_loop(0, trips, pass_a, tuple(ms))

    accs = []
    for g in range(ng):
        p = jnp.exp2(s_ref[g, slot_own] - ms[g])
        acc = jnp.dot(vT_ref[g, c], p.astype(BF16), preferred_element_type=F32)
        p = jnp.exp2(s_ref[g, slot_prev] - jnp.where(keep_prev[g], ms[g], POS_BIG))
        accs.append(acc + jnp.dot(vT_ref[g, jp], p.astype(BF16), preferred_element_type=F32))
    m_far = [ms[g] - far_bias[g] for g in range(ng)]

    def pass_b(t, accs):
        accs = list(accs)
        for u in range(ATTN_UNROLL):
            j = t * ATTN_UNROLL + u
            for g in range(ng):
                keep = selfar_ref[g, pl.ds(j, 1), :] > 0.5
                p = jnp.exp2(s_ref[g, j] - jnp.where(keep, m_far[g], POS_BIG))
                accs[g] = accs[g] + jnp.dot(vT_ref[g, j], p.astype(BF16), preferred_element_type=F32)
        return tuple(accs)

    accs = lax.fori_loop(0, trips, pass_b, tuple(accs))
    for g in range(ng):
        oT_ref[g] = (accs[g][:HEAD_DIM] / accs[g][HEAD_DIM:HEAD_DIM + 1]).astype(oT_ref.dtype)


def _attn_prompt_and_cache_scan(rel_bias, qT, k4, vT4, km3, page_table, cache_t, q_rep):
    nh, _, n = qT.shape
    nb = n // MOBA_BLOCK
    ng = ATTN_HEADS
    assert nb % ATTN_UNROLL == 0 and nh % ng == 0
    bs, pages_per_seq = page_table.shape
    page = cache_t.shape[-1]
    assert MOBA_BLOCK == 2 * page and page == LANES
    nbs = pages_per_seq // 2
    pps = 2 * SCAN_BLOCKS
    scan_steps = pages_per_seq // pps
    assert nbs % (2 * SCAN_BLOCKS) == 0 and MOBA_TOPK <= nbs <= LANES
    assert (nh // ng) * nb == bs * scan_steps

    def scan_pos(h, c):
        step = h * nb + c
        return step // scan_steps, step % scan_steps

    def page_map(h, c, pt, i):
        seq, t = scan_pos(h, c)
        return pt[seq, t * pps + i], 0, 0, 0

    page_specs = [pl.BlockSpec((1, N_HEADS, HEAD_DIM, page), functools.partial(page_map, i=i))
                  for i in range(pps)]
    tile = (ng, MOBA_BLOCK, MOBA_BLOCK)
    return pl.pallas_call(
        _attn_kernel,
        grid_spec=pltpu.PrefetchScalarGridSpec(
            num_scalar_prefetch=1,
            grid=(nh // ng, nb),
            in_specs=[pl.BlockSpec(memory_space=pltpu.SMEM),
                      pl.BlockSpec((ng, HEAD_DIM, MOBA_BLOCK), lambda h, c, pt: (h, 0, c)),
                      pl.BlockSpec((ng, nb, MOBA_BLOCK, HEAD_DIM), lambda h, c, pt: (h, 0, 0, 0)),
                      pl.BlockSpec((ng, nb, V_ROWS, MOBA_BLOCK), lambda h, c, pt: (h, 0, 0, 0)),
                      pl.BlockSpec((ng, nb, HEAD_DIM), lambda h, c, pt: (h, 0, 0))]
                     + page_specs
                     + [pl.BlockSpec((1, N_HEADS, HEAD_DIM, page),
                                     lambda h, c, pt: (scan_pos(h, c)[0], 0, 0, 0))],
            out_specs=[pl.BlockSpec((ng, HEAD_DIM, MOBA_BLOCK), lambda h, c, pt: (h, 0, c)),
                       pl.BlockSpec((1, N_HEADS, 2 * SCAN_BLOCKS, MOBA_BLOCK),
                                    lambda h, c, pt: (scan_pos(h, c)[0], 0, scan_pos(h, c)[1] // 2, 0)),
                       pl.BlockSpec((1, N_HEADS, LANES), lambda h, c, pt: (scan_pos(h, c)[0], 0, 0))],
            scratch_shapes=[pltpu.VMEM(tile, F32),
                            pltpu.VMEM(tile, F32),
                            pltpu.VMEM((ng, nb, MOBA_BLOCK), F32),
                            pltpu.VMEM((ng, nb, MOBA_BLOCK), F32),
                            pltpu.VMEM((ng, nb + 2, MOBA_BLOCK, MOBA_BLOCK), F32),
                            pltpu.VMEM((N_HEADS, LANES), F32)]),
        out_shape=[jax.ShapeDtypeStruct((nh, HEAD_DIM, n), BF16),
                   jax.ShapeDtypeStruct((bs, N_HEADS, nbs, MOBA_BLOCK), F32),
                   jax.ShapeDtypeStruct((bs, N_HEADS, LANES), jnp.int32)],
        compiler_params=_params("arbitrary", "arbitrary"),
    )(page_table, rel_bias, qT, k4, vT4, km3, *([cache_t] * pps), q_rep)


def _cache_scan_step(t, nt, page_refs, q_ref, l_ref, sel_ref, sc_ref):
    q = q_ref[0]
    lane = lax.broadcasted_iota(jnp.int32, sc_ref.shape, 1)

    @pl.when(t == 0)
    def _():
        sc_ref[...] = jnp.zeros_like(sc_ref)

    sc = sc_ref[...]
    logits = []
    for jj in range(SCAN_BLOCKS):
        halves = [jnp.sum(page_refs[2 * jj + i][0] * q, axis=1) for i in range(2)]
        logit = jnp.concatenate(halves, axis=-1)
        logits.append(logit)
        mean = jnp.sum(logit, axis=-1, keepdims=True) * (1.0 / MOBA_BLOCK)
        sc = jnp.where(lane == t * SCAN_BLOCKS + jj, mean, sc)
    sc_ref[...] = sc
    for half in range(2):
        @pl.when(t % 2 == half)
        def _():
            for jj in range(SCAN_BLOCKS):
                l_ref[0, :, half * SCAN_BLOCKS + jj, :] = logits[jj]

    @pl.when(t == nt - 1)
    def _():
        nb = nt * SCAN_BLOCKS
        lane_f = lane.astype(F32)
        _, firsts = _top_blocks(jnp.where(lane < nb, sc, -jnp.inf), lane_f, LANES, 1)
        out = jnp.zeros(sc.shape, F32)
        for r, first in enumerate(firsts):
            out = jnp.where(lane == r, first, out)
        sel_ref[0] = out.astype(jnp.int32)


def _decode_attend_kernel(sel_ref, pt_ref, relb_ref, l_ref, relbt_ref, q_ref, kn_ref, vn_ref, cv_hbm,
                          a_ref, vbuf, sem, rows_ref, blast_ref):
    s = pl.program_id(0)
    ns = pl.num_programs(0)
    nh, nb = l_ref.shape[1:3]
    page = vbuf.shape[-1]
    n_tiles = 2 * MOBA_TOPK
    slot = s % 2

    def tile_copy(seq, h, t, buf_slot):
        blk = sel_ref[seq, h * MOBA_TOPK + t // 2]
        return pltpu.make_async_copy(cv_hbm.at[pt_ref[seq, 2 * blk + t % 2], h],
                                     vbuf.at[buf_slot, h, t], sem.at[buf_slot])

    def start_fetch(seq, buf_slot):
        def per_head(h, carry):
            for t in range(n_tiles):
                tile_copy(seq, h, t, buf_slot).start()
            return carry
        lax.fori_loop(0, nh, per_head, 0)

    @pl.when(s == 0)
    def _():
        start_fetch(0, 0)
        pos = lax.broadcasted_iota(jnp.int32, (1, MOBA_BLOCK), 1)
        for h in range(nh):
            blast_ref[pl.ds(h, 1), :] = _bias_tile(relb_ref, h, MOBA_BLOCK - pos)

    @pl.when(s + 1 < ns)
    def _():
        start_fetch(s + 1, 1 - slot)

    for h in range(nh):
        for r in range(MOBA_TOPK):
            blk = sel_ref[s, h * MOBA_TOPK + r]
            row = l_ref[0, h, pl.ds(blk, 1), :]
            bias = jnp.where(blk == nb - 1, blast_ref[pl.ds(h, 1), :], relb_ref[N_BUCKETS - 1, h])
            rows_ref[r, pl.ds(h, 1), :] = row + bias
    q = q_ref[0]
    s_own = jnp.sum(q * kn_ref[0], axis=-1, keepdims=True) + relbt_ref[:, 0:1]
    m = s_own
    for r in range(MOBA_TOPK):
        m = jnp.maximum(m, jnp.max(rows_ref[r], axis=-1, keepdims=True))
    w_own = jnp.exp(s_own - m)
    ps = [jnp.exp(rows_ref[r] - m) for r in range(MOBA_TOPK)]
    den = w_own
    for p in ps:
        den = den + jnp.sum(p, axis=-1, keepdims=True)
    p_all = jnp.concatenate(ps, axis=-1).astype(BF16)
    num_own = w_own * vn_ref[0]

    def wait_head(h, carry):
        for t in range(n_tiles):
            tile_copy(s, h, t, slot).wait()
        return carry
    lax.fori_loop(0, nh, wait_head, 0)

    for h in range(nh):
        vt = jnp.concatenate([vbuf[slot, h, t] for t in range(n_tiles)], axis=-1).astype(BF16)
        ph = jnp.broadcast_to(p_all[h:h + 1], (8, n_tiles * page))
        pv = lax.dot_general(ph, vt, (((1,), (1,)), ((), ())), preferred_element_type=F32)
        a_ref[0, pl.ds(h, 1), :] = (pv[0:1] + num_own[h:h + 1]) / den[h:h + 1]


def _decode_attend(sel2, page_table, rel_bias, logits, cache_vt, q3, kn3, vn3):
    b, nh, nb, _ = logits.shape
    page = cache_vt.shape[-1]
    tok = pl.BlockSpec((1, nh, HEAD_DIM), lambda s, sel, pt: (s, 0, 0))
    return pl.pallas_call(
        _decode_attend_kernel,
        grid_spec=pltpu.PrefetchScalarGridSpec(
            num_scalar_prefetch=2,
            grid=(b,),
            in_specs=[pl.BlockSpec(memory_space=pltpu.SMEM),
                      pl.BlockSpec((1, nh, nb, MOBA_BLOCK), lambda s, sel, pt: (s, 0, 0, 0)),
                      pl.BlockSpec((nh, N_BUCKETS), lambda s, sel, pt: (0, 0)),
                      tok, tok, tok,
                      pl.BlockSpec(memory_space=pl.ANY)],
            out_specs=tok,
            scratch_shapes=[pltpu.VMEM((2, nh, 2 * MOBA_TOPK, HEAD_DIM, page), F32),
                            pltpu.SemaphoreType.DMA((2,)),
                            pltpu.VMEM((MOBA_TOPK, nh, MOBA_BLOCK), F32),
                            pltpu.VMEM((nh, MOBA_BLOCK), F32)]),
        out_shape=jax.ShapeDtypeStruct((b, nh, HEAD_DIM), F32),
        compiler_params=_params("arbitrary"),
    )(sel2, page_table, rel_bias, logits, rel_bias.T, q3, kn3, vn3, cache_vt)


def kernel(x_prompt, x_sample, cache_k, cache_v, state_pool, page_table, rel_bias, w_in, w_pool,
           pool_scale, w_attn_out, w_o, ln1_g, ln1_b, w_ffn_in, w_ffn_out, ln2_g, ln2_b):
    assert w_in.shape[0] == 1 and x_prompt.shape[0] == 1 and x_sample.shape[1] == 1
    n = x_prompt.shape[1]
    bs = x_sample.shape[0]
    nb = n // MOBA_BLOCK
    page = cache_k.shape[2]
    past_len = page_table.shape[1] * page
    assert past_len % MOBA_BLOCK == 0 and past_len >= max(POOL_WINDOWS)

    w_in_b = w_in[0].astype(BF16)
    w_pool_b = w_pool[0].astype(BF16)
    w_ao_b = w_attn_out[0].astype(BF16)
    w_o_b = w_o[0].astype(BF16)
    w_fi_b = w_ffn_in[0].astype(BF16)
    w_fo_b = w_ffn_out[0].astype(BF16)
    pscale = pool_scale[0][None]
    g1, b1, g2, b2 = ln1_g[0][None], ln1_b[0][None], ln2_g[0][None], ln2_b[0][None]

    xp = x_prompt[0]
    xs = x_sample[:, 0]
    xp_b = xp.astype(BF16)
    xs_b = xs.astype(BF16)
    col_q, col_k, col_v, col_g = D_POOL, D_POOL + D_ATTN, D_POOL + 2 * D_ATTN, D_POOL + 3 * D_ATTN
    hd = (N_HEADS, HEAD_DIM)

    tm, tn = 1024, 1024
    (u_p,) = _proj(xp_b, w_in_b, 0, D_POOL, tm, tn, [_rowmajor_out(n, D_POOL, tm, tn, F32)], _ep_f32)
    (qT_p,) = _proj(xp_b, w_in_b, col_q, D_ATTN, tm, tn,
                    [_transposed_out(n, D_ATTN, tm, tn, BF16)], _ep_qT)
    kT_p, kb_p, km_p = _proj(
        xp_b, w_in_b, col_k, D_ATTN, tm, tn,
        [_transposed_out(n, D_ATTN, tm, tn, F32),
         (jax.ShapeDtypeStruct((N_HEADS, n, HEAD_DIM), BF16),
          pl.BlockSpec((N_HEADS, tm, HEAD_DIM), lambda j, i: (j, i, 0))),
         (jax.ShapeDtypeStruct((n // tm, tm // MOBA_BLOCK, D_ATTN), F32),
          pl.BlockSpec((1, tm // MOBA_BLOCK, tn), lambda j, i: (i, 0, j)))],
        _ep_k)
    vT_p, vb_p = _proj(
        xp_b, w_in_b, col_v, D_ATTN, tm, tn,
        [_transposed_out(n, D_ATTN, tm, tn, F32),
         (jax.ShapeDtypeStruct((N_HEADS, nb, V_ROWS, MOBA_BLOCK), BF16),
          pl.BlockSpec((N_HEADS, tm // MOBA_BLOCK, V_ROWS, MOBA_BLOCK), lambda j, i: (j, i, 0, 0)))],
        _ep_v)
    (gates_p,) = _proj(xp_b, w_in_b, col_g, 2 * D_MODEL, tm, tn,
                       [_rowmajor_out(n, 2 * D_MODEL, tm, tn, BF16)], _ep_gate)

    row32 = lambda nc, dt: [_rowmajor_out(bs, nc, bs, tn, dt)]
    (u_s,) = _proj(xs_b, w_in_b, 0, D_POOL, bs, tn, row32(D_POOL, F32), _ep_f32)
    (q_s,) = _proj(xs_b, w_in_b, col_q, D_ATTN, bs, tn, row32(D_ATTN, F32), _ep_q_scaled)
    (k_s,) = _proj(xs_b, w_in_b, col_k, D_ATTN, bs, tn, row32(D_ATTN, F32), _ep_f32)
    (v_s,) = _proj(xs_b, w_in_b, col_v, D_ATTN, bs, tn, row32(D_ATTN, F32), _ep_f32)
    (gates_s,) = _proj(xs_b, w_in_b, col_g, 2 * D_MODEL, bs, tn, row32(2 * D_MODEL, BF16), _ep_gate)

    ck_t = cache_k[0].transpose(0, 2, 3, 1)
    cv_t = cache_v[0].transpose(0, 2, 3, 1)
    q_rep = jnp.broadcast_to(q_s.reshape(bs, *hd, 1), (bs, *hd, page))
    km3 = km_p.reshape(nb, N_HEADS, HEAD_DIM).transpose(1, 0, 2)
    aT, logits_s, sel_s = _attn_prompt_and_cache_scan(
        rel_bias, qT_p.reshape(N_HEADS, HEAD_DIM, n), kb_p.reshape(N_HEADS, nb, MOBA_BLOCK, HEAD_DIM),
        vb_p, km3, page_table, ck_t, q_rep)

    pooled_p = _pool_prompt(u_p, 512)
    h1_p, h1b_p = _mix(pooled_p, aT.reshape(D_ATTN, n), gates_p, xp, w_pool_b, pscale, w_ao_b, w_o_b,
                       g1, b1, 256, True)
    y_p = _ffn(h1b_p, h1_p, w_fi_b, w_fo_b, g2, b2, 512, 512)

    sel2 = sel_s[:, :, :MOBA_TOPK].reshape(bs, N_HEADS * MOBA_TOPK)
    a_s = _decode_attend(sel2, page_table, rel_bias, logits_s, cv_t,
                         q_s.reshape(bs, *hd), k_s.reshape(bs, *hd), v_s.reshape(bs, *hd))
    a_s = a_s.reshape(bs, D_ATTN).astype(BF16)

    state_t = state_pool[0].transpose(1, 0, 2)
    pooled_s = _pool_sample(state_t, u_s)
    h1_s, h1b_s = _mix(pooled_s, a_s, gates_s, xs, w_pool_b, pscale, w_ao_b, w_o_b, g1, b1, bs, False)
    y_s = _ffn(h1b_s, h1_s, w_fi_b, w_fo_b, g2, b2, bs, 512)

    new_pool_p = u_p[n - POOL_STATE:][None, None]
    new_pool_s = jnp.concatenate([state_t[1:], u_s[None]], axis=0).transpose(1, 0, 2)[None]
    k_out = kT_p.reshape(*hd, n).transpose(2, 0, 1)[None, None]
    v_out = vT_p.reshape(*hd, n).transpose(2, 0, 1)[None, None]
    return (y_p[None], y_s[:, None], k_out, v_out,
            k_s.reshape(1, bs, 1, *hd), v_s.reshape(1, bs, 1, *hd),
            new_pool_p, new_pool_s)
```

```python
import functools
import math

import numpy as np
import jax
import jax.numpy as jnp
from jax import lax
from jax.experimental import pallas as pl
from jax.experimental.pallas import tpu as pltpu

D_MODEL = 2048
N_HEADS = 16
HEAD_DIM = 64
D_ATTN = N_HEADS * HEAD_DIM
MOBA_BLOCK = 256
MOBA_TOPK = 3
POOL_WINDOWS = (2, 4, 8, 16)
N_POOL_GROUPS = len(POOL_WINDOWS)
POOL_GROUP_DIM = 256
D_POOL = N_POOL_GROUPS * POOL_GROUP_DIM
POOL_OUT_GROUP = D_MODEL // N_POOL_GROUPS
POOL_STATE = max(POOL_WINDOWS) - 1
POOL_HALO = 16
N_BUCKETS = 32
MAX_DISTANCE = 128
D_FF = -(-8 * D_MODEL // (3 * 256)) * 256
ALPHA = 2.0 ** 0.25
LN_EPS = 1e-5
NEG_BIG = -1e30
POS_BIG = 1e30
LANES = 128
V_ROWS = 80
ATTN_UNROLL = 4
ATTN_HEADS = 4
MAX_SHIFT_EXCESS = 60.0
LOG2E = 1.4426950408889634
SCAN_BLOCKS = 8
VMEM_LIMIT = 56 * 1024 * 1024

F32 = jnp.float32
BF16 = jnp.bfloat16


def _t5_bucket_np(rel):
    n = np.maximum(rel, 0)
    max_exact = N_BUCKETS // 2
    nf = np.maximum(n, 1).astype(np.float32)
    large = max_exact + (np.log(nf / max_exact) / math.log(MAX_DISTANCE / max_exact)
                         * (N_BUCKETS - max_exact)).astype(np.int32)
    large = np.minimum(large, N_BUCKETS - 1)
    return np.where(n < max_exact, n, large)


_BUCKETS = _t5_bucket_np(np.arange(0, 4 * MOBA_BLOCK))
BUCKET_START = tuple(int(np.argmax(_BUCKETS >= t)) for t in range(N_BUCKETS))
FAR_REL = BUCKET_START[N_BUCKETS - 1]
assert FAR_REL <= MOBA_BLOCK


def _params(*sem):
    return pltpu.CompilerParams(dimension_semantics=sem, vmem_limit_bytes=VMEM_LIMIT)


def _layer_norm(y, g, b):
    mu = jnp.mean(y, axis=-1, keepdims=True)
    d = y - mu
    var = jnp.mean(d * d, axis=-1, keepdims=True)
    return d * lax.rsqrt(var + LN_EPS) * g + b


def _bias_tile(relb_ref, h, rel):
    b = jnp.full(rel.shape, relb_ref[0, h], F32)
    for t in range(1, N_BUCKETS):
        b = jnp.where(rel >= BUCKET_START[t], relb_ref[t, h], b)
    return b


def _top_blocks(score, idx, n_idx, axis):
    chosen = jnp.zeros(score.shape, F32)
    firsts = []
    left = score
    for _ in range(MOBA_TOPK):
        mx = jnp.max(left, axis=axis, keepdims=True)
        first = jnp.min(jnp.where(left == mx, idx, float(n_idx)), axis=axis, keepdims=True)
        hit = idx == first
        chosen = jnp.where(hit, 1.0, chosen)
        left = jnp.where(hit, -jnp.inf, left)
        firsts.append(first)
    return chosen, firsts


def _proj_kernel(x_ref, w_ref, *out_refs, epilogue):
    z = jnp.dot(x_ref[...], w_ref[...], preferred_element_type=F32)
    epilogue(z, *out_refs)


def _proj(xb, wb, col0, n_cols, tm, tn, outs, epilogue):
    m, k = xb.shape
    assert m % tm == 0 and n_cols % tn == 0 and col0 % tn == 0
    return pl.pallas_call(
        functools.partial(_proj_kernel, epilogue=epilogue),
        grid=(n_cols // tn, m // tm),
        in_specs=[pl.BlockSpec((tm, k), lambda j, i: (i, 0)),
                  pl.BlockSpec((k, tn), lambda j, i: (0, col0 // tn + j))],
        out_specs=[o[1] for o in outs],
        out_shape=[o[0] for o in outs],
        compiler_params=_params("arbitrary", "arbitrary"),
    )(xb, wb)


def _rowmajor_out(m, n_cols, tm, tn, dtype):
    return (jax.ShapeDtypeStruct((m, n_cols), dtype), pl.BlockSpec((tm, tn), lambda j, i: (i, j)))


def _transposed_out(m, n_cols, tm, tn, dtype):
    return (jax.ShapeDtypeStruct((n_cols, m), dtype), pl.BlockSpec((tn, tm), lambda j, i: (j, i)))


def _ep_f32(z, o_ref):
    o_ref[...] = z


def _ep_q_scaled(z, o_ref):
    o_ref[...] = z * (1.0 / math.sqrt(HEAD_DIM))


def _ep_qT(z, oT_ref):
    oT_ref[...] = (z * (1.0 / math.sqrt(HEAD_DIM))).T.astype(BF16)


def _ep_k(z, kT_ref, kb_ref, km_ref):
    tm, tn = z.shape
    kT_ref[...] = z.T
    zb = z.astype(BF16)
    for h in range(tn // HEAD_DIM):
        kb_ref[h] = zb[:, h * HEAD_DIM:(h + 1) * HEAD_DIM]
    km_ref[0] = jnp.mean(z.reshape(tm // MOBA_BLOCK, MOBA_BLOCK, tn), axis=1)


def _ep_v(z, vT_ref, vb_ref):
    tm, tn = z.shape
    zT = z.T
    vT_ref[...] = zT
    zTb = zT.astype(BF16)
    pad_rows = V_ROWS - HEAD_DIM
    ones_row = jnp.where(lax.broadcasted_iota(jnp.int32, (pad_rows, MOBA_BLOCK), 0) == 0,
                         1.0, 0.0).astype(BF16)
    for h in range(tn // HEAD_DIM):
        for jb in range(tm // MOBA_BLOCK):
            vb_ref[h, jb, 0:HEAD_DIM, :] = zTb[h * HEAD_DIM:(h + 1) * HEAD_DIM,
                                               jb * MOBA_BLOCK:(jb + 1) * MOBA_BLOCK]
            vb_ref[h, jb, HEAD_DIM:V_ROWS, :] = ones_row


def _ep_gate(z, o_ref):
    o_ref[...] = jax.nn.sigmoid(z).astype(BF16)


def _pool_prompt_kernel(u_ref, halo_ref, o_ref):
    i = pl.program_id(0)
    tm = u_ref.shape[0]
    u = u_ref[...]
    halo = jnp.where(i > 0, halo_ref[...], 0.0)
    ext = jnp.concatenate([halo, u], axis=0)
    pos = i * tm + lax.broadcasted_iota(jnp.int32, (tm, 1), 0)
    for g, w in enumerate(POOL_WINDOWS):
        cols = slice(g * POOL_GROUP_DIM, (g + 1) * POOL_GROUP_DIM)
        s = ext[:, cols]
        span = 1
        while span < w:
            s = s + pltpu.roll(s, span, axis=0)
            span *= 2
        cnt = jnp.minimum(pos + 1, w).astype(F32)
        o_ref[:, cols] = (s[POOL_HALO:] / cnt - u[:, cols]).astype(o_ref.dtype)


def _pool_prompt(u, tm):
    m = u.shape[0]
    hb = tm // POOL_HALO
    return pl.pallas_call(
        _pool_prompt_kernel,
        grid=(m // tm,),
        in_specs=[pl.BlockSpec((tm, D_POOL), lambda i: (i, 0)),
                  pl.BlockSpec((POOL_HALO, D_POOL), lambda i: (jnp.maximum(i * hb - 1, 0), 0))],
        out_specs=pl.BlockSpec((tm, D_POOL), lambda i: (i, 0)),
        out_shape=jax.ShapeDtypeStruct((m, D_POOL), BF16),
        compiler_params=_params("arbitrary"),
    )(u, u)


def _pool_sample_kernel(st_ref, u_ref, o_ref):
    u = u_ref[...]
    for g, w in enumerate(POOL_WINDOWS):
        cols = slice(g * POOL_GROUP_DIM, (g + 1) * POOL_GROUP_DIM)
        win = u[:, cols]
        for t in range(POOL_STATE - (w - 1), POOL_STATE):
            win = win + st_ref[t, :, cols]
        o_ref[:, cols] = (win / float(w) - u[:, cols]).astype(o_ref.dtype)


def _pool_sample(state_t, u):
    b = u.shape[0]
    return pl.pallas_call(
        _pool_sample_kernel,
        out_shape=jax.ShapeDtypeStruct((b, D_POOL), BF16),
        compiler_params=pltpu.CompilerParams(vmem_limit_bytes=VMEM_LIMIT),
    )(state_t, u)


def _mix_kernel(pooled_ref, a_ref, gp_ref, ga_ref, x_ref, wpool_ref, pscale_ref, wao_ref,
                wo_ref, g_ref, b_ref, h_ref, hb_ref, *, a_transposed):
    parts = [jnp.dot(pooled_ref[:, g * POOL_GROUP_DIM:(g + 1) * POOL_GROUP_DIM], wpool_ref[g],
                     preferred_element_type=F32) for g in range(N_POOL_GROUPS)]
    p_br = jnp.concatenate(parts, axis=-1) * pscale_ref[...]
    if a_transposed:
        a_br = lax.dot_general(a_ref[...], wao_ref[...], (((0,), (0,)), ((), ())),
                               preferred_element_type=F32)
    else:
        a_br = jnp.dot(a_ref[...], wao_ref[...], preferred_element_type=F32)
    merged = gp_ref[...].astype(F32) * p_br + ga_ref[...].astype(F32) * a_br
    mix = jnp.dot(merged.astype(BF16), wo_ref[...], preferred_element_type=F32)
    h = _layer_norm(ALPHA * x_ref[...] + mix, g_ref[...], b_ref[...])
    h_ref[...] = h
    hb_ref[...] = h.astype(BF16)


def _mix(pooled, a, gates, x, wpool, pscale, wao, wo, ln_g, ln_b, tm, a_transposed):
    m = x.shape[0]
    row = lambda i: (i, 0)
    const2 = lambda i: (0, 0)
    a_spec = (pl.BlockSpec((D_ATTN, tm), lambda i: (0, i)) if a_transposed
              else pl.BlockSpec((tm, D_ATTN), row))
    return pl.pallas_call(
        functools.partial(_mix_kernel, a_transposed=a_transposed),
        grid=(m // tm,),
        in_specs=[pl.BlockSpec((tm, D_POOL), row),
                  a_spec,
                  pl.BlockSpec((tm, D_MODEL), lambda i: (i, 0)),
                  pl.BlockSpec((tm, D_MODEL), lambda i: (i, 1)),
                  pl.BlockSpec((tm, D_MODEL), row),
                  pl.BlockSpec((N_POOL_GROUPS, POOL_GROUP_DIM, POOL_OUT_GROUP), lambda i: (0, 0, 0)),
                  pl.BlockSpec((1, D_MODEL), const2),
                  pl.BlockSpec((D_ATTN, D_MODEL), const2),
                  pl.BlockSpec((D_MODEL, D_MODEL), const2),
                  pl.BlockSpec((1, D_MODEL), const2),
                  pl.BlockSpec((1, D_MODEL), const2)],
        out_specs=[pl.BlockSpec((tm, D_MODEL), row), pl.BlockSpec((tm, D_MODEL), row)],
        out_shape=[jax.ShapeDtypeStruct((m, D_MODEL), F32), jax.ShapeDtypeStruct((m, D_MODEL), BF16)],
        compiler_params=_params("arbitrary"),
    )(pooled, a, gates, gates, x, wpool, pscale, wao, wo, ln_g, ln_b)


def _ffn_kernel(hb_ref, h_ref, wg_ref, wu_ref, wo_ref, g_ref, b_ref, o_ref, acc_ref):
    f = pl.program_id(1)

    @pl.when(f == 0)
    def _():
        acc_ref[...] = jnp.zeros_like(acc_ref)

    hb = hb_ref[...]
    gate = jnp.dot(hb, wg_ref[...], preferred_element_type=F32)
    up = jnp.dot(hb, wu_ref[...], preferred_element_type=F32)
    act = (gate * jax.nn.sigmoid(gate) * up).astype(BF16)
    acc_ref[...] += jnp.dot(act, wo_ref[...], preferred_element_type=F32)

    @pl.when(f == pl.num_programs(1) - 1)
    def _():
        o_ref[...] = _layer_norm(ALPHA * h_ref[...] + acc_ref[...], g_ref[...], b_ref[...])


def _ffn(hb, h, w_in, w_out, ln_g, ln_b, tm, tf):
    m = h.shape[0]
    nf = D_FF // tf
    row = lambda i, f: (i, 0)
    const2 = lambda i, f: (0, 0)
    return pl.pallas_call(
        _ffn_kernel,
        grid=(m // tm, nf),
        in_specs=[pl.BlockSpec((tm, D_MODEL), row),
                  pl.BlockSpec((tm, D_MODEL), row),
                  pl.BlockSpec((D_MODEL, tf), lambda i, f: (0, f)),
                  pl.BlockSpec((D_MODEL, tf), lambda i, f: (0, nf + f)),
                  pl.BlockSpec((tf, D_MODEL), lambda i, f: (f, 0)),
                  pl.BlockSpec((1, D_MODEL), const2),
                  pl.BlockSpec((1, D_MODEL), const2)],
        out_specs=pl.BlockSpec((tm, D_MODEL), row),
        out_shape=jax.ShapeDtypeStruct((m, D_MODEL), F32),
        scratch_shapes=[pltpu.VMEM((tm, D_MODEL), F32)],
        compiler_params=_params("arbitrary", "arbitrary"),
    )(hb, h, w_in, w_in, w_out, ln_g, ln_b)


def _attn_kernel(pt_ref, relb_ref, qT_ref, k_ref, vT_ref, km_ref, *refs):
    n_pages = 2 * SCAN_BLOCKS
    page_refs = refs[:n_pages]
    (qrep_ref, oT_ref, l_ref, selo_ref,
     bown_ref, bprev_ref, sel_ref, selfar_ref, s_ref, sc_ref) = refs[n_pages:]
    scan_steps = pt_ref.shape[1] // n_pages
    step = pl.program_id(0) * pl.num_programs(1) + pl.program_id(1)

    c = pl.program_id(1)
    ng, nb = k_ref.shape[:2]
    heads = [pl.program_id(0) * ng + g for g in range(ng)]
    blk = MOBA_BLOCK

    @pl.when(c == 0)
    def _():
        kk = lax.broadcasted_iota(jnp.int32, (blk, blk), 0)
        qq = lax.broadcasted_iota(jnp.int32, (blk, blk), 1)
        rel = qq - kk
        for g, h in enumerate(heads):
            bown_ref[g] = jnp.where(rel >= 0, _bias_tile(relb_ref, h, rel), NEG_BIG)
            bprev_ref[g] = _bias_tile(relb_ref, h, rel + blk)
        sc_ref[...] = jnp.zeros_like(sc_ref)

    _cache_scan_step(step % scan_steps, scan_steps, page_refs, qrep_ref, l_ref, selo_ref, sc_ref)

    qTs = [qT_ref[g] for g in range(ng)]

    jidx = lax.broadcasted_iota(jnp.int32, (nb, blk), 0).astype(F32)
    cf = c.astype(F32)
    for g in range(ng):
        km = km_ref[g]
        km_hi = km.astype(BF16)
        r1 = km - km_hi.astype(F32)
        km_mid = r1.astype(BF16)
        km_lo = (r1 - km_mid.astype(F32)).astype(BF16)
        score = (jnp.dot(km_hi, qTs[g], preferred_element_type=F32)
                 + jnp.dot(km_mid, qTs[g], preferred_element_type=F32)
                 + jnp.dot(km_lo, qTs[g], preferred_element_type=F32))
        chosen, _ = _top_blocks(jnp.where(jidx < cf, score, -jnp.inf), jidx, nb, 0)
        sel_ref[g] = jnp.where(jidx < cf, chosen, 0.0)
        selfar_ref[g] = jnp.where(jidx < cf - 1.0, chosen, 0.0)

    n_far = jnp.maximum(c - 1, 0)
    trips = (n_far + ATTN_UNROLL - 1) // ATTN_UNROLL
    far_bias = [relb_ref[N_BUCKETS - 1, h] * LOG2E for h in heads]
    jp = jnp.maximum(c - 1, 0)
    keep_prev = [sel_ref[g, pl.ds(jp, 1), :] > 0.5 for g in range(ng)]

    def col_max(s):
        return jnp.max(s, axis=0, keepdims=True)

    m_own = []
    excess_near = []
    for g in range(ng):
        s_own = (jnp.dot(k_ref[g, c], qTs[g], preferred_element_type=F32) + bown_ref[g]) * LOG2E
        s_ref[g, 0] = s_own
        s_prev = (jnp.dot(k_ref[g, jp], qTs[g], preferred_element_type=F32) + bprev_ref[g]) * LOG2E
        s_ref[g, 1] = s_prev
        m_own.append(col_max(s_own))
        excess_near.append(jnp.where(keep_prev[g], col_max(s_prev) - m_own[g], NEG_BIG))

    def attend(shift):
        accs = []
        for g in range(ng):
            p = jnp.exp2(s_ref[g, 0] - shift[g])
            acc = jnp.dot(vT_ref[g, c], p.astype(BF16), preferred_element_type=F32)
            p = jnp.exp2(s_ref[g, 1] - jnp.where(keep_prev[g], shift[g], POS_BIG))
            accs.append(acc + jnp.dot(vT_ref[g, jp], p.astype(BF16), preferred_element_type=F32))
        shift_far = [shift[g] - far_bias[g] for g in range(ng)]
        m_rel = [m_own[g] - far_bias[g] for g in range(ng)]

        def body(t, carry):
            accs, excess = list(carry[0]), list(carry[1])
            tiles = [(t * ATTN_UNROLL + u, g) for u in range(ATTN_UNROLL) for g in range(ng)]
            logits = [jnp.dot(k_ref[g, j], qTs[g], preferred_element_type=F32) * LOG2E for j, g in tiles]
            for (j, g), s in zip(tiles, logits):
                keep = selfar_ref[g, pl.ds(j, 1), :] > 0.5
                excess[g] = jnp.maximum(excess[g], jnp.where(keep, col_max(s) - m_rel[g], NEG_BIG))
                p = jnp.exp2(s - jnp.where(keep, shift_far[g], POS_BIG))
                accs[g] = accs[g] + jnp.dot(vT_ref[g, j], p.astype(BF16), preferred_element_type=F32)
            return tuple(accs), tuple(excess)

        return lax.fori_loop(0, trips, body, (tuple(accs), tuple(excess_near)))

    def write_out(accs):
        for g in range(ng):
            oT_ref[g] = (accs[g][:HEAD_DIM] / accs[g][HEAD_DIM:HEAD_DIM + 1]).astype(oT_ref.dtype)

    accs, excess = attend(m_own)
    write_out(accs)
    worst = excess[0]
    for g in range(1, ng):
        worst = jnp.maximum(worst, excess[g])

    @pl.when(jnp.max(worst) > MAX_SHIFT_EXCESS)
    def _():
        accs, _ = attend([m_own[g] + jnp.maximum(excess[g], 0.0) for g in range(ng)])
        write_out(accs)


def _attn_prompt_and_cache_scan(rel_bias, qT, k4, vT4, km3, page_table, cache_t, q_rep):
    nh, _, n = qT.shape
    nb = n // MOBA_BLOCK
    ng = ATTN_HEADS
    assert nb % ATTN_UNROLL == 0 and nh % ng == 0
    bs, pages_per_seq = page_table.shape
    page = cache_t.shape[-1]
    assert MOBA_BLOCK == 2 * page and page == LANES
    nbs = pages_per_seq // 2
    pps = 2 * SCAN_BLOCKS
    scan_steps = pages_per_seq // pps
    assert nbs % SCAN_BLOCKS == 0 and MOBA_TOPK <= SCAN_BLOCKS and nbs <= LANES
    assert (nh // ng) * nb == bs * scan_steps

    def scan_pos(h, c):
        step = h * nb + c
        return step // scan_steps, step % scan_steps

    def page_map(h, c, pt, i):
        seq, t = scan_pos(h, c)
        return pt[seq, t * pps + i], 0, 0, 0

    page_specs = [pl.BlockSpec((1, N_HEADS, HEAD_DIM, page), functools.partial(page_map, i=i))
                  for i in range(pps)]
    tile = (ng, MOBA_BLOCK, MOBA_BLOCK)
    return pl.pallas_call(
        _attn_kernel,
        grid_spec=pltpu.PrefetchScalarGridSpec(
            num_scalar_prefetch=1,
            grid=(nh // ng, nb),
            in_specs=[pl.BlockSpec(memory_space=pltpu.SMEM),
                      pl.BlockSpec((ng, HEAD_DIM, MOBA_BLOCK), lambda h, c, pt: (h, 0, c)),
                      pl.BlockSpec((ng, nb, MOBA_BLOCK, HEAD_DIM), lambda h, c, pt: (h, 0, 0, 0)),
                      pl.BlockSpec((ng, nb, V_ROWS, MOBA_BLOCK), lambda h, c, pt: (h, 0, 0, 0)),
                      pl.BlockSpec((ng, nb, HEAD_DIM), lambda h, c, pt: (h, 0, 0))]
                     + page_specs
                     + [pl.BlockSpec((1, N_HEADS, HEAD_DIM, page),
                                     lambda h, c, pt: (scan_pos(h, c)[0], 0, 0, 0))],
            out_specs=[pl.BlockSpec((ng, HEAD_DIM, MOBA_BLOCK), lambda h, c, pt: (h, 0, c)),
                       pl.BlockSpec((1, 1, N_HEADS, SCAN_BLOCKS * MOBA_BLOCK),
                                    lambda h, c, pt: (*scan_pos(h, c), 0, 0)),
                       pl.BlockSpec((1, N_HEADS, LANES), lambda h, c, pt: (scan_pos(h, c)[0], 0, 0))],
            scratch_shapes=[pltpu.VMEM(tile, F32),
                            pltpu.VMEM(tile, F32),
                            pltpu.VMEM((ng, nb, MOBA_BLOCK), F32),
                            pltpu.VMEM((ng, nb, MOBA_BLOCK), F32),
                            pltpu.VMEM((ng, 2, MOBA_BLOCK, MOBA_BLOCK), F32),
                            pltpu.VMEM((N_HEADS, LANES), F32)]),
        out_shape=[jax.ShapeDtypeStruct((nh, HEAD_DIM, n), BF16),
                   jax.ShapeDtypeStruct((bs, scan_steps, N_HEADS, SCAN_BLOCKS * MOBA_BLOCK), F32),
                   jax.ShapeDtypeStruct((bs, N_HEADS, LANES), jnp.int32)],
        compiler_params=_params("arbitrary", "arbitrary"),
    )(page_table, rel_bias, qT, k4, vT4, km3, *([cache_t] * pps), q_rep)


def _cache_scan_step(t, nt, page_refs, q_ref, l_ref, sel_ref, sc_ref):
    q = q_ref[0]
    lane = lax.broadcasted_iota(jnp.int32, sc_ref.shape, 1)
    n_seen = (t + 1) * SCAN_BLOCKS
    sc = sc_ref[...]
    for jj in range(SCAN_BLOCKS):
        halves = [jnp.sum(page_refs[2 * jj + i][0] * q, axis=1) for i in range(2)]
        logit = jnp.concatenate(halves, axis=-1)
        l_ref[0, 0, :, jj * MOBA_BLOCK:(jj + 1) * MOBA_BLOCK] = logit
        mean = jnp.sum(logit, axis=-1, keepdims=True) * (1.0 / MOBA_BLOCK)
        sc = jnp.where(lane == t * SCAN_BLOCKS + jj, mean, sc)
    sc_ref[...] = sc
    _, firsts = _top_blocks(jnp.where(lane < n_seen, sc, -jnp.inf), lane.astype(F32), LANES, 1)
    out = jnp.zeros(sc.shape, F32)
    for r, first in enumerate(firsts):
        out = jnp.where(lane == r, first, out)
    sel_ref[0] = out.astype(jnp.int32)


def _decode_attend_kernel(sel_ref, pt_ref, relb_ref, l_ref, relbt_ref, q_ref, kn_ref, vn_ref, cv_hbm,
                          a_ref, vbuf, sem, rows_ref, blast_ref):
    s = pl.program_id(0)
    ns = pl.num_programs(0)
    nh = l_ref.shape[2]
    nb = l_ref.shape[1] * SCAN_BLOCKS
    page = vbuf.shape[-1]
    n_tiles = 2 * MOBA_TOPK
    slot = s % 2

    def tile_copy(seq, h, t, buf_slot):
        blk = sel_ref[seq, h * MOBA_TOPK + t // 2]
        return pltpu.make_async_copy(cv_hbm.at[pt_ref[seq, 2 * blk + t % 2], h],
                                     vbuf.at[buf_slot, h, t], sem.at[buf_slot])

    def start_fetch(seq, buf_slot):
        def per_head(h, carry):
            for t in range(n_tiles):
                tile_copy(seq, h, t, buf_slot).start()
            return carry
        lax.fori_loop(0, nh, per_head, 0)

    @pl.when(s == 0)
    def _():
        start_fetch(0, 0)
        pos = lax.broadcasted_iota(jnp.int32, (1, MOBA_BLOCK), 1)
        for h in range(nh):
            blast_ref[pl.ds(h, 1), :] = _bias_tile(relb_ref, h, MOBA_BLOCK - pos)

    @pl.when(s + 1 < ns)
    def _():
        start_fetch(s + 1, 1 - slot)

    for h in range(nh):
        for r in range(MOBA_TOPK):
            blk = sel_ref[s, h * MOBA_TOPK + r]
            wide = l_ref[0, blk // SCAN_BLOCKS, pl.ds(h, 1), :]
            row = wide[:, :MOBA_BLOCK]
            for seg in range(1, SCAN_BLOCKS):
                row = jnp.where(blk % SCAN_BLOCKS == seg, wide[:, seg * MOBA_BLOCK:(seg + 1) * MOBA_BLOCK], row)
            bias =jnp.where(blk == nb - 1, blast_ref[pl.ds(h, 1), :], relb_ref[N_BUCKETS - 1, h])
            rows_ref[r, pl.ds(h, 1), :] = row + bias
    q = q_ref[0]
    s_own = jnp.sum(q * kn_ref[0], axis=-1, keepdims=True) + relbt_ref[:, 0:1]
    m = s_own
    for r in range(MOBA_TOPK):
        m = jnp.maximum(m, jnp.max(rows_ref[r], axis=-1, keepdims=True))
    w_own = jnp.exp(s_own - m)
    ps = [jnp.exp(rows_ref[r] - m) for r in range(MOBA_TOPK)]
    den = w_own
    for p in ps:
        den = den + jnp.sum(p, axis=-1, keepdims=True)
    p_all = jnp.concatenate(ps, axis=-1).astype(BF16)
    num_own = w_own * vn_ref[0]

    def wait_head(h, carry):
        for t in range(n_tiles):
            tile_copy(s, h, t, slot).wait()
        return carry
    lax.fori_loop(0, nh, wait_head, 0)

    for h in range(nh):
        vt = jnp.concatenate([vbuf[slot, h, t] for t in range(n_tiles)], axis=-1).astype(BF16)
        ph = jnp.broadcast_to(p_all[h:h + 1], (8, n_tiles * page))
        pv = lax.dot_general(ph, vt, (((1,), (1,)), ((), ())), preferred_element_type=F32)
        a_ref[0, pl.ds(h, 1), :] = (pv[0:1] + num_own[h:h + 1]) / den[h:h + 1]


def _decode_attend(sel2, page_table, rel_bias, logits, cache_vt, q3, kn3, vn3):
    b, n_groups, nh, width = logits.shape
    page = cache_vt.shape[-1]
    tok = pl.BlockSpec((1, nh, HEAD_DIM), lambda s, sel, pt: (s, 0, 0))
    return pl.pallas_call(
        _decode_attend_kernel,
        grid_spec=pltpu.PrefetchScalarGridSpec(
            num_scalar_prefetch=2,
            grid=(b,),
            in_specs=[pl.BlockSpec(memory_space=pltpu.SMEM),
                      pl.BlockSpec((1, n_groups, nh, width), lambda s, sel, pt: (s, 0, 0, 0)),
                      pl.BlockSpec((nh, N_BUCKETS), lambda s, sel, pt: (0, 0)),
                      tok, tok, tok,
                      pl.BlockSpec(memory_space=pl.ANY)],
            out_specs=tok,
            scratch_shapes=[pltpu.VMEM((2, nh, 2 * MOBA_TOPK, HEAD_DIM, page), F32),
                            pltpu.SemaphoreType.DMA((2,)),
                            pltpu.VMEM((MOBA_TOPK, nh, MOBA_BLOCK), F32),
                            pltpu.VMEM((nh, MOBA_BLOCK), F32)]),
        out_shape=jax.ShapeDtypeStruct((b, nh, HEAD_DIM), F32),
        compiler_params=_params("arbitrary"),
    )(sel2, page_table, rel_bias, logits, rel_bias.T, q3, kn3, vn3, cache_vt)


def kernel(x_prompt, x_sample, cache_k, cache_v, state_pool, page_table, rel_bias, w_in, w_pool,
           pool_scale, w_attn_out, w_o, ln1_g, ln1_b, w_ffn_in, w_ffn_out, ln2_g, ln2_b):
    assert w_in.shape[0] == 1 and x_prompt.shape[0] == 1 and x_sample.shape[1] == 1
    n = x_prompt.shape[1]
    bs = x_sample.shape[0]
    nb = n // MOBA_BLOCK
    page = cache_k.shape[2]
    past_len = page_table.shape[1] * page
    assert past_len % MOBA_BLOCK == 0 and past_len >= max(POOL_WINDOWS)

    w_in_b = w_in[0].astype(BF16)
    w_pool_b = w_pool[0].astype(BF16)
    w_ao_b = w_attn_out[0].astype(BF16)
    w_o_b = w_o[0].astype(BF16)
    w_fi_b = w_ffn_in[0].astype(BF16)
    w_fo_b = w_ffn_out[0].astype(BF16)
    pscale = pool_scale[0][None]
    g1, b1, g2, b2 = ln1_g[0][None], ln1_b[0][None], ln2_g[0][None], ln2_b[0][None]

    xp = x_prompt[0]
    xs = x_sample[:, 0]
    xp_b = xp.astype(BF16)
    xs_b = xs.astype(BF16)
    col_q, col_k, col_v, col_g = D_POOL, D_POOL + D_ATTN, D_POOL + 2 * D_ATTN, D_POOL + 3 * D_ATTN
    hd = (N_HEADS, HEAD_DIM)

    tm, tn = 1024, 1024
    (u_p,) = _proj(xp_b, w_in_b, 0, D_POOL, tm, tn, [_rowmajor_out(n, D_POOL, tm, tn, F32)], _ep_f32)
    (qT_p,) = _proj(xp_b, w_in_b, col_q, D_ATTN, tm, tn,
                    [_transposed_out(n, D_ATTN, tm, tn, BF16)], _ep_qT)
    kT_p, kb_p, km_p = _proj(
        xp_b, w_in_b, col_k, D_ATTN, tm, tn,
        [_transposed_out(n, D_ATTN, tm, tn, F32),
         (jax.ShapeDtypeStruct((N_HEADS, n, HEAD_DIM), BF16),
          pl.BlockSpec((N_HEADS, tm, HEAD_DIM), lambda j, i: (j, i, 0))),
         (jax.ShapeDtypeStruct((n // tm, tm // MOBA_BLOCK, D_ATTN), F32),
          pl.BlockSpec((1, tm // MOBA_BLOCK, tn), lambda j, i: (i, 0, j)))],
        _ep_k)
    vT_p, vb_p = _proj(
        xp_b, w_in_b, col_v, D_ATTN, tm, tn,
        [_transposed_out(n, D_ATTN, tm, tn, F32),
         (jax.ShapeDtypeStruct((N_HEADS, nb, V_ROWS, MOBA_BLOCK), BF16),
          pl.BlockSpec((N_HEADS, tm // MOBA_BLOCK, V_ROWS, MOBA_BLOCK), lambda j, i: (j, i, 0, 0)))],
        _ep_v)
    (gates_p,) = _proj(xp_b, w_in_b, col_g, 2 * D_MODEL, tm, tn,
                       [_rowmajor_out(n, 2 * D_MODEL, tm, tn, BF16)], _ep_gate)

    row32 = lambda nc, dt: [_rowmajor_out(bs, nc, bs, tn, dt)]
    (u_s,) = _proj(xs_b, w_in_b, 0, D_POOL, bs, tn, row32(D_POOL, F32), _ep_f32)
    (q_s,) = _proj(xs_b, w_in_b, col_q, D_ATTN, bs, tn, row32(D_ATTN, F32), _ep_q_scaled)
    (k_s,) = _proj(xs_b, w_in_b, col_k, D_ATTN, bs, tn, row32(D_ATTN, F32), _ep_f32)
    (v_s,) = _proj(xs_b, w_in_b, col_v, D_ATTN, bs, tn, row32(D_ATTN, F32), _ep_f32)
    (gates_s,) = _proj(xs_b, w_in_b, col_g, 2 * D_MODEL, bs, tn, row32(2 * D_MODEL, BF16), _ep_gate)

    ck_t = cache_k[0].transpose(0, 2, 3, 1)
    cv_t = cache_v[0].transpose(0, 2, 3, 1)
    q_rep = jnp.broadcast_to(q_s.reshape(bs, *hd, 1), (bs, *hd, page))
    km3 = km_p.reshape(nb, N_HEADS, HEAD_DIM).transpose(1, 0, 2)
    aT, logits_s, sel_s = _attn_prompt_and_cache_scan(
        rel_bias, qT_p.reshape(N_HEADS, HEAD_DIM, n), kb_p.reshape(N_HEADS, nb, MOBA_BLOCK, HEAD_DIM),
        vb_p, km3, page_table, ck_t, q_rep)

    pooled_p = _pool_prompt(u_p, 512)
    h1_p, h1b_p = _mix(pooled_p, aT.reshape(D_ATTN, n), gates_p, xp, w_pool_b, pscale, w_ao_b, w_o_b,
                       g1, b1, 256, True)
    y_p = _ffn(h1b_p, h1_p, w_fi_b, w_fo_b, g2, b2, 512, 512)

    sel2 = sel_s[:, :, :MOBA_TOPK].reshape(bs, N_HEADS * MOBA_TOPK)
    a_s = _decode_attend(sel2, page_table, rel_bias, logits_s, cv_t,
                         q_s.reshape(bs, *hd), k_s.reshape(bs, *hd), v_s.reshape(bs, *hd))
    a_s = a_s.reshape(bs, D_ATTN).astype(BF16)

    state_t = state_pool[0].transpose(1, 0, 2)
    pooled_s = _pool_sample(state_t, u_s)
    h1_s, h1b_s = _mix(pooled_s, a_s, gates_s, xs, w_pool_b, pscale, w_ao_b, w_o_b, g1, b1, bs, False)
    y_s = _ffn(h1b_s, h1_s, w_fi_b, w_fo_b, g2, b2, bs, 512)

    new_pool_p = u_p[n - POOL_STATE:][None, None]
    new_pool_s = jnp.concatenate([state_t[1:], u_s[None]], axis=0).transpose(1, 0, 2)[None]
    k_out = kT_p.reshape(*hd, n).transpose(2, 0, 1)[None, None]
    v_out = vT_p.reshape(*hd, n).transpose(2, 0, 1)[None, None]
    return (y_p[None], y_s[:, None], k_out, v_out,
            k_s.reshape(1, bs, 1, *hd), v_s.reshape(1, bs, 1, *hd),
            new_pool_p, new_pool_s)
```

```python
import functools
import math

import numpy as np
import jax
import jax.numpy as jnp
from jax import lax
from jax.experimental import pallas as pl
from jax.experimental.pallas import tpu as pltpu

D_MODEL = 2048
N_HEADS = 16
HEAD_DIM = 64
D_ATTN = N_HEADS * HEAD_DIM
MOBA_BLOCK = 256
MOBA_TOPK = 3
POOL_WINDOWS = (2, 4, 8, 16)
N_POOL_GROUPS = len(POOL_WINDOWS)
POOL_GROUP_DIM = 256
D_POOL = N_POOL_GROUPS * POOL_GROUP_DIM
POOL_OUT_GROUP = D_MODEL // N_POOL_GROUPS
POOL_STATE = max(POOL_WINDOWS) - 1
POOL_HALO = 16
N_BUCKETS = 32
MAX_DISTANCE = 128
D_FF = -(-8 * D_MODEL // (3 * 256)) * 256
ALPHA = 2.0 ** 0.25
LN_EPS = 1e-5
NEG_BIG = -1e30
POS_BIG = 1e30
LANES = 128
V_ROWS = 80
ATTN_UNROLL = 4
ATTN_HEADS = 4
MAX_SHIFT_EXCESS = 60.0
LOG2E = 1.4426950408889634
SCAN_BLOCKS = 8
VMEM_LIMIT = 56 * 1024 * 1024

F32 = jnp.float32
BF16 = jnp.bfloat16


def _t5_bucket_np(rel):
    n = np.maximum(rel, 0)
    max_exact = N_BUCKETS // 2
    nf = np.maximum(n, 1).astype(np.float32)
    large = max_exact + (np.log(nf / max_exact) / math.log(MAX_DISTANCE / max_exact)
                         * (N_BUCKETS - max_exact)).astype(np.int32)
    large = np.minimum(large, N_BUCKETS - 1)
    return np.where(n < max_exact, n, large)


_BUCKETS = _t5_bucket_np(np.arange(0, 4 * MOBA_BLOCK))
BUCKET_START = tuple(int(np.argmax(_BUCKETS >= t)) for t in range(N_BUCKETS))
FAR_REL = BUCKET_START[N_BUCKETS - 1]
assert FAR_REL <= MOBA_BLOCK


def _params(*sem):
    return pltpu.CompilerParams(dimension_semantics=sem, vmem_limit_bytes=VMEM_LIMIT)


def _layer_norm(y, g, b):
    mu = jnp.mean(y, axis=-1, keepdims=True)
    d = y - mu
    var = jnp.mean(d * d, axis=-1, keepdims=True)
    return d * lax.rsqrt(var + LN_EPS) * g + b


def _bias_tile(relb_ref, h, rel):
    b = jnp.full(rel.shape, relb_ref[0, h], F32)
    for t in range(1, N_BUCKETS):
        b = jnp.where(rel >= BUCKET_START[t], relb_ref[t, h], b)
    return b


def _top_blocks(score, idx, n_idx, axis):
    chosen = jnp.zeros(score.shape, F32)
    firsts = []
    left = score
    for _ in range(MOBA_TOPK):
        mx = jnp.max(left, axis=axis, keepdims=True)
        first = jnp.min(jnp.where(left == mx, idx, float(n_idx)), axis=axis, keepdims=True)
        hit = idx == first
        chosen = jnp.where(hit, 1.0, chosen)
        left = jnp.where(hit, -jnp.inf, left)
        firsts.append(first)
    return chosen, firsts


def _proj_kernel(x_ref, w_ref, *out_refs, epilogue):
    z = jnp.dot(x_ref[...], w_ref[...], preferred_element_type=F32)
    epilogue(z, *out_refs)


def _proj(x, wb, col0, n_cols, tm, tn, outs, epilogue):
    m, k = x.shape
    assert m % tm == 0 and n_cols % tn == 0 and col0 % tn == 0
    return pl.pallas_call(
        functools.partial(_proj_kernel, epilogue=epilogue),
        grid=(n_cols // tn, m // tm),
        in_specs=[pl.BlockSpec((tm, k), lambda j, i: (i, 0)),
                  pl.BlockSpec((k, tn), lambda j, i: (0, col0 // tn + j))],
        out_specs=[o[1] for o in outs],
        out_shape=[o[0] for o in outs],
        compiler_params=_params("arbitrary", "arbitrary"),
    )(x, wb)


def _rowmajor_out(m, n_cols, tm, tn, dtype):
    return (jax.ShapeDtypeStruct((m, n_cols), dtype), pl.BlockSpec((tm, tn), lambda j, i: (i, j)))


def _transposed_out(m, n_cols, tm, tn, dtype):
    return (jax.ShapeDtypeStruct((n_cols, m), dtype), pl.BlockSpec((tn, tm), lambda j, i: (j, i)))


def _ep_f32(z, o_ref):
    o_ref[...] = z


def _ep_q_scaled(z, o_ref):
    o_ref[...] = z * (1.0 / math.sqrt(HEAD_DIM))


def _ep_qT(z, oT_ref):
    oT_ref[...] = (z * (1.0 / math.sqrt(HEAD_DIM))).T.astype(BF16)


def _ep_k(z, kT_ref, kb_ref, km_ref):
    tm, tn = z.shape
    kT_ref[...] = z.T
    zb = z.astype(BF16)
    for h in range(tn // HEAD_DIM):
        kb_ref[h] = zb[:, h * HEAD_DIM:(h + 1) * HEAD_DIM]
    km_ref[0] = jnp.mean(z.reshape(tm // MOBA_BLOCK, MOBA_BLOCK, tn), axis=1)


def _ep_v(z, vT_ref, vb_ref):
    tm, tn = z.shape
    zT = z.T
    vT_ref[...] = zT
    zTb = zT.astype(BF16)
    pad_rows = V_ROWS - HEAD_DIM
    ones_row = jnp.where(lax.broadcasted_iota(jnp.int32, (pad_rows, MOBA_BLOCK), 0) == 0,
                         1.0, 0.0).astype(BF16)
    for h in range(tn // HEAD_DIM):
        for jb in range(tm // MOBA_BLOCK):
            vb_ref[h, jb, 0:HEAD_DIM, :] = zTb[h * HEAD_DIM:(h + 1) * HEAD_DIM,
                                               jb * MOBA_BLOCK:(jb + 1) * MOBA_BLOCK]
            vb_ref[h, jb, HEAD_DIM:V_ROWS, :] = ones_row


def _ep_gate(z, o_ref):
    o_ref[...] = jax.nn.sigmoid(z).astype(BF16)


def _pool_prompt_kernel(u_ref, halo_ref, o_ref):
    i = pl.program_id(0)
    tm = u_ref.shape[0]
    u = u_ref[...]
    halo = jnp.where(i > 0, halo_ref[...], 0.0)
    ext = jnp.concatenate([halo, u], axis=0)
    pos = i * tm + lax.broadcasted_iota(jnp.int32, (tm, 1), 0)
    for g, w in enumerate(POOL_WINDOWS):
        cols = slice(g * POOL_GROUP_DIM, (g + 1) * POOL_GROUP_DIM)
        s = ext[:, cols]
        span = 1
        while span < w:
            s = s + pltpu.roll(s, span, axis=0)
            span *= 2
        cnt = jnp.minimum(pos + 1, w).astype(F32)
        o_ref[:, cols] = (s[POOL_HALO:] / cnt - u[:, cols]).astype(o_ref.dtype)


def _pool_prompt(u, tm):
    m = u.shape[0]
    hb = tm // POOL_HALO
    return pl.pallas_call(
        _pool_prompt_kernel,
        grid=(m // tm,),
        in_specs=[pl.BlockSpec((tm, D_POOL), lambda i: (i, 0)),
                  pl.BlockSpec((POOL_HALO, D_POOL), lambda i: (jnp.maximum(i * hb - 1, 0), 0))],
        out_specs=pl.BlockSpec((tm, D_POOL), lambda i: (i, 0)),
        out_shape=jax.ShapeDtypeStruct((m, D_POOL), BF16),
        compiler_params=_params("arbitrary"),
    )(u, u)


def _pool_sample_kernel(st_ref, u_ref, o_ref):
    u = u_ref[...]
    for g, w in enumerate(POOL_WINDOWS):
        cols = slice(g * POOL_GROUP_DIM, (g + 1) * POOL_GROUP_DIM)
        win = u[:, cols]
        for t in range(POOL_STATE - (w - 1), POOL_STATE):
            win = win + st_ref[t, :, cols]
        o_ref[:, cols] = (win / float(w) - u[:, cols]).astype(o_ref.dtype)


def _pool_sample(state_t, u):
    b = u.shape[0]
    return pl.pallas_call(
        _pool_sample_kernel,
        out_shape=jax.ShapeDtypeStruct((b, D_POOL), BF16),
        compiler_params=pltpu.CompilerParams(vmem_limit_bytes=VMEM_LIMIT),
    )(state_t, u)


def _mix_kernel(pooled_ref, a_ref, gp_ref, ga_ref, x_ref, wpool_ref, pscale_ref, wao_ref,
                wo_ref, g_ref, b_ref, h_ref, hb_ref, *, a_transposed):
    parts = [jnp.dot(pooled_ref[:, g * POOL_GROUP_DIM:(g + 1) * POOL_GROUP_DIM], wpool_ref[g],
                     preferred_element_type=F32) for g in range(N_POOL_GROUPS)]
    p_br = jnp.concatenate(parts, axis=-1) * pscale_ref[...]
    if a_transposed:
        a_br = lax.dot_general(a_ref[...], wao_ref[...], (((0,), (0,)), ((), ())),
                               preferred_element_type=F32)
    else:
        a_br = jnp.dot(a_ref[...], wao_ref[...], preferred_element_type=F32)
    merged = gp_ref[...].astype(F32) * p_br + ga_ref[...].astype(F32) * a_br
    mix = jnp.dot(merged.astype(BF16), wo_ref[...], preferred_element_type=F32)
    h = _layer_norm(ALPHA * x_ref[...] + mix, g_ref[...], b_ref[...])
    h_ref[...] = h
    hb_ref[...] = h.astype(BF16)


def _mix(pooled, a, gates, x, wpool, pscale, wao, wo, ln_g, ln_b, tm, a_transposed):
    m = x.shape[0]
    row = lambda i: (i, 0)
    const2 = lambda i: (0, 0)
    a_spec = (pl.BlockSpec((D_ATTN, tm), lambda i: (0, i)) if a_transposed
              else pl.BlockSpec((tm, D_ATTN), row))
    return pl.pallas_call(
        functools.partial(_mix_kernel, a_transposed=a_transposed),
        grid=(m // tm,),
        in_specs=[pl.BlockSpec((tm, D_POOL), row),
                  a_spec,
                  pl.BlockSpec((tm, D_MODEL), lambda i: (i, 0)),
                  pl.BlockSpec((tm, D_MODEL), lambda i: (i, 1)),
                  pl.BlockSpec((tm, D_MODEL), row),
                  pl.BlockSpec((N_POOL_GROUPS, POOL_GROUP_DIM, POOL_OUT_GROUP), lambda i: (0, 0, 0)),
                  pl.BlockSpec((1, D_MODEL), const2),
                  pl.BlockSpec((D_ATTN, D_MODEL), const2),
                  pl.BlockSpec((D_MODEL, D_MODEL), const2),
                  pl.BlockSpec((1, D_MODEL), const2),
                  pl.BlockSpec((1, D_MODEL), const2)],
        out_specs=[pl.BlockSpec((tm, D_MODEL), row), pl.BlockSpec((tm, D_MODEL), row)],
        out_shape=[jax.ShapeDtypeStruct((m, D_MODEL), F32), jax.ShapeDtypeStruct((m, D_MODEL), BF16)],
        compiler_params=_params("arbitrary"),
    )(pooled, a, gates, gates, x, wpool, pscale, wao, wo, ln_g, ln_b)


def _ffn_kernel(hb_ref, h_ref, wg_ref, wu_ref, wo_ref, g_ref, b_ref, o_ref, acc_ref):
    f = pl.program_id(1)

    @pl.when(f == 0)
    def _():
        acc_ref[...] = jnp.zeros_like(acc_ref)

    hb = hb_ref[...]
    gate = jnp.dot(hb, wg_ref[...], preferred_element_type=F32)
    up = jnp.dot(hb, wu_ref[...], preferred_element_type=F32)
    act = (gate * jax.nn.sigmoid(gate) * up).astype(BF16)
    acc_ref[...] += jnp.dot(act, wo_ref[...], preferred_element_type=F32)

    @pl.when(f == pl.num_programs(1) - 1)
    def _():
        o_ref[...] = _layer_norm(ALPHA * h_ref[...] + acc_ref[...], g_ref[...], b_ref[...])


def _ffn(hb, h, w_in, w_out, ln_g, ln_b, tm, tf):
    m = h.shape[0]
    nf = D_FF // tf
    row = lambda i, f: (i, 0)
    const2 = lambda i, f: (0, 0)
    return pl.pallas_call(
        _ffn_kernel,
        grid=(m // tm, nf),
        in_specs=[pl.BlockSpec((tm, D_MODEL), row),
                  pl.BlockSpec((tm, D_MODEL), row),
                  pl.BlockSpec((D_MODEL, tf), lambda i, f: (0, f)),
                  pl.BlockSpec((D_MODEL, tf), lambda i, f: (0, nf + f)),
                  pl.BlockSpec((tf, D_MODEL), lambda i, f: (f, 0)),
                  pl.BlockSpec((1, D_MODEL), const2),
                  pl.BlockSpec((1, D_MODEL), const2)],
        out_specs=pl.BlockSpec((tm, D_MODEL), row),
        out_shape=jax.ShapeDtypeStruct((m, D_MODEL), F32),
        scratch_shapes=[pltpu.VMEM((tm, D_MODEL), F32)],
        compiler_params=_params("arbitrary", "arbitrary"),
    )(hb, h, w_in, w_in, w_out, ln_g, ln_b)


def _attn_kernel(pt_ref, relb_ref, qT_ref, k_ref, vT_ref, km_ref, *refs):
    n_pages = 2 * SCAN_BLOCKS
    page_refs = refs[:n_pages]
    qrep_ref, oT_ref, l_ref, selo_ref, bias_ref, sel_ref, sc_ref = refs[n_pages:]
    scan_steps = pt_ref.shape[1] // n_pages
    step = pl.program_id(0) * pl.num_programs(1) + pl.program_id(1)

    c = pl.program_id(1)
    ng, nb = k_ref.shape[:2]
    heads = [pl.program_id(0) * ng + g for g in range(ng)]
    blk = MOBA_BLOCK
    prev, own = 0, 1

    @pl.when(c == 0)
    def _():
        kk = lax.broadcasted_iota(jnp.int32, (blk, blk), 0)
        qq = lax.broadcasted_iota(jnp.int32, (blk, blk), 1)
        rel = qq - kk
        for g, h in enumerate(heads):
            bias_ref[g, prev] = _bias_tile(relb_ref, h, rel + blk)
            bias_ref[g, own] = jnp.where(rel >= 0, _bias_tile(relb_ref, h, rel), NEG_BIG)
        sc_ref[...] = jnp.zeros_like(sc_ref)

    _cache_scan_step(step % scan_steps, scan_steps, page_refs, qrep_ref, l_ref, selo_ref, sc_ref)

    qTs = [qT_ref[g] for g in range(ng)]

    jidx = lax.broadcasted_iota(jnp.int32, (nb, blk), 0).astype(F32)
    cf = c.astype(F32)
    for g in range(ng):
        km = km_ref[g]
        km_hi = km.astype(BF16)
        r1 = km - km_hi.astype(F32)
        km_mid = r1.astype(BF16)
        km_lo = (r1 - km_mid.astype(F32)).astype(BF16)
        score = (jnp.dot(km_hi, qTs[g], preferred_element_type=F32)
                 + jnp.dot(km_mid, qTs[g], preferred_element_type=F32)
                 + jnp.dot(km_lo, qTs[g], preferred_element_type=F32))
        chosen, _ = _top_blocks(jnp.where(jidx < cf, score, -jnp.inf), jidx, nb, 0)
        sel_ref[g] = jnp.where(jidx < cf, chosen, jnp.where(jidx == cf, 1.0, 0.0))

    n_far = jnp.maximum(c - 1, 0)
    trips = (n_far + ATTN_UNROLL - 1) // ATTN_UNROLL
    far_bias = [relb_ref[N_BUCKETS - 1, h] * LOG2E for h in heads]
    jp = jnp.maximum(c - 1, 0)

    def col_max(s):
        return jnp.max(s, axis=0, keepdims=True)

    def visit(s, keep, v_tile, shift, acc, top):
        top = jnp.maximum(top, jnp.where(keep, col_max(s), NEG_BIG))
        p = jnp.exp2(s - jnp.where(keep, shift, POS_BIG))
        return acc + jnp.dot(v_tile, p.astype(BF16), preferred_element_type=F32), top

    def attend(shift):
        accs = [jnp.zeros((V_ROWS, blk), F32)] * ng
        tops = [jnp.full((1, blk), NEG_BIG, F32)] * ng
        near = [(c, own, None), (jp, prev, c > 0)]
        logits = [(jnp.dot(k_ref[g, j], qTs[g], preferred_element_type=F32) + bias_ref[g, kind]) * LOG2E
                  for j, kind, _ in near for g in range(ng)]
        for i, (j, _, allowed) in enumerate(near):
            for g in range(ng):
                keep = sel_ref[g, pl.ds(j, 1), :] > 0.5
                keep = keep if allowed is None else jnp.logical_and(keep, allowed)
                accs[g], tops[g] = visit(logits[i * ng + g], keep, vT_ref[g, j], shift[g], accs[g], tops[g])

        def body(t, carry):
            accs, tops = list(carry[0]), list(carry[1])
            tiles = [(t * ATTN_UNROLL + u, g) for u in range(ATTN_UNROLL) for g in range(ng)]
            logits = [jnp.dot(k_ref[g, j], qTs[g], preferred_element_type=F32) * LOG2E for j, g in tiles]
            for (j, g), s in zip(tiles, logits):
                keep = jnp.logical_and(sel_ref[g, pl.ds(j, 1), :] > 0.5, j < n_far)
                accs[g], top = visit(s, keep, vT_ref[g, j], shift[g] - far_bias[g], accs[g],
                                     tops[g] - far_bias[g])
                tops[g] = top + far_bias[g]
            return tuple(accs), tuple(tops)

        return lax.fori_loop(0, trips, body, (tuple(accs), tuple(tops)))

    def write_out(accs):
        for g in range(ng):
            oT_ref[g] = (accs[g][:HEAD_DIM] / accs[g][HEAD_DIM:HEAD_DIM + 1]).astype(oT_ref.dtype)

    accs, tops = attend([jnp.zeros((1, blk), F32)] * ng)
    write_out(accs)
    off = jnp.abs(tops[0])
    for g in range(1, ng):
        off = jnp.maximum(off, jnp.abs(tops[g]))

    @pl.when(jnp.max(off) > MAX_SHIFT_EXCESS)
    def _():
        accs, _ = attend(tops)
        write_out(accs)


def _attn_prompt_and_cache_scan(rel_bias, qT, k4, vT4, km3, page_table, cache_t, q_rep):
    nh, _, n = qT.shape
    nb = n // MOBA_BLOCK
    ng = ATTN_HEADS
    assert nb % ATTN_UNROLL == 0 and nh % ng == 0
    bs, pages_per_seq = page_table.shape
    page = cache_t.shape[-1]
    assert MOBA_BLOCK == 2 * page and page == LANES
    nbs = pages_per_seq // 2
    pps = 2 * SCAN_BLOCKS
    scan_steps = pages_per_seq // pps
    assert nbs % SCAN_BLOCKS == 0 and MOBA_TOPK <= SCAN_BLOCKS and nbs <= LANES
    assert (nh // ng) * nb == bs * scan_steps

    def scan_pos(h, c):
        step = h * nb + c
        return step // scan_steps, step % scan_steps

    def page_map(h, c, pt, i):
        seq, t = scan_pos(h, c)
        return pt[seq, t * pps + i], 0, 0, 0

    page_specs = [pl.BlockSpec((1, N_HEADS, HEAD_DIM, page), functools.partial(page_map, i=i))
                  for i in range(pps)]
    return pl.pallas_call(
        _attn_kernel,
        grid_spec=pltpu.PrefetchScalarGridSpec(
            num_scalar_prefetch=1,
            grid=(nh // ng, nb),
            in_specs=[pl.BlockSpec(memory_space=pltpu.SMEM),
                      pl.BlockSpec((ng, HEAD_DIM, MOBA_BLOCK), lambda h, c, pt: (h, 0, c)),
                      pl.BlockSpec((ng, nb, MOBA_BLOCK, HEAD_DIM), lambda h, c, pt: (h, 0, 0, 0)),
                      pl.BlockSpec((ng, nb, V_ROWS, MOBA_BLOCK), lambda h, c, pt: (h, 0, 0, 0)),
                      pl.BlockSpec((ng, nb, HEAD_DIM), lambda h, c, pt: (h, 0, 0))]
                     + page_specs
                     + [pl.BlockSpec((1, N_HEADS, HEAD_DIM, page),
                                     lambda h, c, pt: (scan_pos(h, c)[0], 0, 0, 0))],
            out_specs=[pl.BlockSpec((ng, HEAD_DIM, MOBA_BLOCK), lambda h, c, pt: (h, 0, c)),
                       pl.BlockSpec((1, 1, N_HEADS, SCAN_BLOCKS * MOBA_BLOCK),
                                    lambda h, c, pt: (*scan_pos(h, c), 0, 0)),
                       pl.BlockSpec((1, N_HEADS, LANES), lambda h, c, pt: (scan_pos(h, c)[0], 0, 0))],
            scratch_shapes=[pltpu.VMEM((ng, 2, MOBA_BLOCK, MOBA_BLOCK), F32),
                            pltpu.VMEM((ng, nb, MOBA_BLOCK), F32),
                            pltpu.VMEM((N_HEADS, LANES), F32)]),
        out_shape=[jax.ShapeDtypeStruct((nh, HEAD_DIM, n), BF16),
                   jax.ShapeDtypeStruct((bs, scan_steps, N_HEADS, SCAN_BLOCKS * MOBA_BLOCK), F32),
                   jax.ShapeDtypeStruct((bs, N_HEADS, LANES), jnp.int32)],
        compiler_params=_params("arbitrary", "arbitrary"),
    )(page_table, rel_bias, qT, k4, vT4, km3, *([cache_t] * pps), q_rep)


def _cache_scan_step(t, nt, page_refs, q_ref, l_ref, sel_ref, sc_ref):
    q = q_ref[0]
    lane = lax.broadcasted_iota(jnp.int32, sc_ref.shape, 1)
    n_seen = (t + 1) * SCAN_BLOCKS
    sc = sc_ref[...]
    for jj in range(SCAN_BLOCKS):
        halves = [jnp.sum(page_refs[2 * jj + i][0] * q, axis=1) for i in range(2)]
        logit = jnp.concatenate(halves, axis=-1)
        l_ref[0, 0, :, jj * MOBA_BLOCK:(jj + 1) * MOBA_BLOCK] = logit
        mean = jnp.sum(logit, axis=-1, keepdims=True) * (1.0 / MOBA_BLOCK)
        sc = jnp.where(lane == t * SCAN_BLOCKS + jj, mean, sc)
    sc_ref[...] = sc
    _, firsts = _top_blocks(jnp.where(lane < n_seen, sc, -jnp.inf), lane.astype(F32), LANES, 1)
    out = jnp.zeros(sc.shape, F32)
    for r, first in enumerate(firsts):
        out = jnp.where(lane == r, first, out)
    sel_ref[0] = out.astype(jnp.int32)


def _decode_attend_kernel(sel_ref, pt_ref, relb_ref, l_ref, relbt_ref, q_ref, kn_ref, vn_ref, cv_hbm,
                          a_ref, vbuf, sem, rows_ref, blast_ref):
    s = pl.program_id(0)
    ns = pl.num_programs(0)
    nh = l_ref.shape[2]
    nb = l_ref.shape[1] * SCAN_BLOCKS
    page = vbuf.shape[-1]
    n_tiles = 2 * MOBA_TOPK
    slot = s % 2

    def tile_copy(seq, h, t, buf_slot):
        blk = sel_ref[seq, h * MOBA_TOPK + t // 2]
        return pltpu.make_async_copy(cv_hbm.at[pt_ref[seq, 2 * blk + t % 2], h],
                                     vbuf.at[buf_slot, h, t], sem.at[buf_slot])

    def start_fetch(seq, buf_slot):
        def per_head(h, carry):
            for t in range(n_tiles):
                tile_copy(seq, h, t, buf_slot).start()
            return carry
        lax.fori_loop(0, nh, per_head, 0)

    @pl.when(s == 0)
    def _():
        start_fetch(0, 0)
        pos = lax.broadcasted_iota(jnp.int32, (1, MOBA_BLOCK), 1)
        for h in range(nh):
            blast_ref[pl.ds(h, 1), :] = _bias_tile(relb_ref, h, MOBA_BLOCK - pos)

    @pl.when(s + 1 < ns)
    def _():
        start_fetch(s + 1, 1 - slot)

    for h in range(nh):
        for r in range(MOBA_TOPK):
            blk = sel_ref[s, h * MOBA_TOPK + r]
            wide = l_ref[0, blk // SCAN_BLOCKS, pl.ds(h, 1), :]
            row = wide[:, :MOBA_BLOCK]
            for seg in range(1, SCAN_BLOCKS):
                row = jnp.where(blk % SCAN_BLOCKS == seg, wide[:, seg * MOBA_BLOCK:(seg + 1) * MOBA_BLOCK], row)
            bias =jnp.where(blk == nb - 1, blast_ref[pl.ds(h, 1), :], relb_ref[N_BUCKETS - 1, h])
            rows_ref[r, pl.ds(h, 1), :] = row + bias
    q = q_ref[0]
    s_own = jnp.sum(q * kn_ref[0], axis=-1, keepdims=True) + relbt_ref[:, 0:1]
    m = s_own
    for r in range(MOBA_TOPK):
        m = jnp.maximum(m, jnp.max(rows_ref[r], axis=-1, keepdims=True))
    w_own = jnp.exp(s_own - m)
    ps = [jnp.exp(rows_ref[r] - m) for r in range(MOBA_TOPK)]
    den = w_own
    for p in ps:
        den = den + jnp.sum(p, axis=-1, keepdims=True)
    p_all = jnp.concatenate(ps, axis=-1).astype(BF16)
    num_own = w_own * vn_ref[0]

    def wait_head(h, carry):
        for t in range(n_tiles):
            tile_copy(s, h, t, slot).wait()
        return carry
    lax.fori_loop(0, nh, wait_head, 0)

    for h in range(nh):
        vt = jnp.concatenate([vbuf[slot, h, t] for t in range(n_tiles)], axis=-1).astype(BF16)
        ph = jnp.broadcast_to(p_all[h:h + 1], (8, n_tiles * page))
        pv = lax.dot_general(ph, vt, (((1,), (1,)), ((), ())), preferred_element_type=F32)
        a_ref[0, pl.ds(h, 1), :] = (pv[0:1] + num_own[h:h + 1]) / den[h:h + 1]


def _decode_attend(sel2, page_table, rel_bias, logits, cache_vt, q3, kn3, vn3):
    b, n_groups, nh, width = logits.shape
    page = cache_vt.shape[-1]
    tok = pl.BlockSpec((1, nh, HEAD_DIM), lambda s, sel, pt: (s, 0, 0))
    return pl.pallas_call(
        _decode_attend_kernel,
        grid_spec=pltpu.PrefetchScalarGridSpec(
            num_scalar_prefetch=2,
            grid=(b,),
            in_specs=[pl.BlockSpec(memory_space=pltpu.SMEM),
                      pl.BlockSpec((1, n_groups, nh, width), lambda s, sel, pt: (s, 0, 0, 0)),
                      pl.BlockSpec((nh, N_BUCKETS), lambda s, sel, pt: (0, 0)),
                      tok, tok, tok,
                      pl.BlockSpec(memory_space=pl.ANY)],
            out_specs=tok,
            scratch_shapes=[pltpu.VMEM((2, nh, 2 * MOBA_TOPK, HEAD_DIM, page), F32),
                            pltpu.SemaphoreType.DMA((2,)),
                            pltpu.VMEM((MOBA_TOPK, nh, MOBA_BLOCK), F32),
                            pltpu.VMEM((nh, MOBA_BLOCK), F32)]),
        out_shape=jax.ShapeDtypeStruct((b, nh, HEAD_DIM), F32),
        compiler_params=_params("arbitrary"),
    )(sel2, page_table, rel_bias, logits, rel_bias.T, q3, kn3, vn3, cache_vt)


def kernel(x_prompt, x_sample, cache_k, cache_v, state_pool, page_table, rel_bias, w_in, w_pool,
           pool_scale, w_attn_out, w_o, ln1_g, ln1_b, w_ffn_in, w_ffn_out, ln2_g, ln2_b):
    assert w_in.shape[0] == 1 and x_prompt.shape[0] == 1 and x_sample.shape[1] == 1
    n = x_prompt.shape[1]
    bs = x_sample.shape[0]
    nb = n // MOBA_BLOCK
    page = cache_k.shape[2]
    past_len = page_table.shape[1] * page
    assert past_len % MOBA_BLOCK == 0 and past_len >= max(POOL_WINDOWS)

    w_in_b = w_in[0].astype(BF16)
    w_pool_b = w_pool[0].astype(BF16)
    w_ao_b = w_attn_out[0].astype(BF16)
    w_o_b = w_o[0].astype(BF16)
    tf = 512
    w_fi_b = w_ffn_in[0].astype(BF16)
    w_fo_b = w_ffn_out[0].astype(BF16)
    pscale = pool_scale[0][None]
    g1, b1, g2, b2 = ln1_g[0][None], ln1_b[0][None], ln2_g[0][None], ln2_b[0][None]

    xp = x_prompt[0]
    xs = x_sample[:, 0]
    xp_b = xp.astype(BF16)
    xs_b = xs.astype(BF16)
    col_q, col_k, col_v, col_g = D_POOL, D_POOL + D_ATTN, D_POOL + 2 * D_ATTN, D_POOL + 3 * D_ATTN
    hd = (N_HEADS, HEAD_DIM)

    tm, tn = 1024, 1024
    (u_p,) = _proj(xp_b, w_in_b, 0, D_POOL, tm, tn, [_rowmajor_out(n, D_POOL, tm, tn, F32)], _ep_f32)
    (qT_p,) = _proj(xp_b, w_in_b, col_q, D_ATTN, tm, tn,
                    [_transposed_out(n, D_ATTN, tm, tn, BF16)], _ep_qT)
    kT_p, kb_p, km_p = _proj(
        xp_b, w_in_b, col_k, D_ATTN, tm, tn,
        [_transposed_out(n, D_ATTN, tm, tn, F32),
         (jax.ShapeDtypeStruct((N_HEADS, n, HEAD_DIM), BF16),
          pl.BlockSpec((N_HEADS, tm, HEAD_DIM), lambda j, i: (j, i, 0))),
         (jax.ShapeDtypeStruct((n // tm, tm // MOBA_BLOCK, D_ATTN), F32),
          pl.BlockSpec((1, tm // MOBA_BLOCK, tn), lambda j, i: (i, 0, j)))],
        _ep_k)
    vT_p, vb_p = _proj(
        xp_b, w_in_b, col_v, D_ATTN, tm, tn,
        [_transposed_out(n, D_ATTN, tm, tn, F32),
         (jax.ShapeDtypeStruct((N_HEADS, nb, V_ROWS, MOBA_BLOCK), BF16),
          pl.BlockSpec((N_HEADS, tm // MOBA_BLOCK, V_ROWS, MOBA_BLOCK), lambda j, i: (j, i, 0, 0)))],
        _ep_v)
    (gates_p,) = _proj(xp_b, w_in_b, col_g, 2 * D_MODEL, tm, tn,
                       [_rowmajor_out(n, 2 * D_MODEL, tm, tn, BF16)], _ep_gate)

    row32 = lambda nc, dt: [_rowmajor_out(bs, nc, bs, tn, dt)]
    (u_s,) = _proj(xs_b, w_in_b, 0, D_POOL, bs, tn, row32(D_POOL, F32), _ep_f32)
    (q_s,) = _proj(xs_b, w_in_b, col_q, D_ATTN, bs, tn, row32(D_ATTN, F32), _ep_q_scaled)
    (k_s,) = _proj(xs_b, w_in_b, col_k, D_ATTN, bs, tn, row32(D_ATTN, F32), _ep_f32)
    (v_s,) = _proj(xs_b, w_in_b, col_v, D_ATTN, bs, tn, row32(D_ATTN, F32), _ep_f32)
    (gates_s,) = _proj(xs_b, w_in_b, col_g, 2 * D_MODEL, bs, tn, row32(2 * D_MODEL, BF16), _ep_gate)

    ck_t = cache_k[0].transpose(0, 2, 3, 1)
    cv_t = cache_v[0].transpose(0, 2, 3, 1)
    q_rep = jnp.broadcast_to(q_s.reshape(bs, *hd, 1), (bs, *hd, page))
    km3 = km_p.reshape(nb, N_HEADS, HEAD_DIM).transpose(1, 0, 2)
    aT, logits_s, sel_s = _attn_prompt_and_cache_scan(
        rel_bias, qT_p.reshape(N_HEADS, HEAD_DIM, n), kb_p.reshape(N_HEADS, nb, MOBA_BLOCK, HEAD_DIM),
        vb_p, km3, page_table, ck_t, q_rep)

    pooled_p = _pool_prompt(u_p, 512)
    h1_p, h1b_p = _mix(pooled_p, aT.reshape(D_ATTN, n), gates_p, xp, w_pool_b, pscale, w_ao_b, w_o_b,
                       g1, b1, 256, True)
    y_p = _ffn(h1b_p, h1_p, w_fi_b, w_fo_b, g2, b2, 512, tf)

    sel2 = sel_s[:, :, :MOBA_TOPK].reshape(bs, N_HEADS * MOBA_TOPK)
    a_s = _decode_attend(sel2, page_table, rel_bias, logits_s, cv_t,
                         q_s.reshape(bs, *hd), k_s.reshape(bs, *hd), v_s.reshape(bs, *hd))
    a_s = a_s.reshape(bs, D_ATTN).astype(BF16)

    state_t = state_pool[0].transpose(1, 0, 2)
    pooled_s = _pool_sample(state_t, u_s)
    h1_s, h1b_s = _mix(pooled_s, a_s, gates_s, xs, w_pool_b, pscale, w_ao_b, w_o_b, g1, b1, bs, False)
    y_s = _ffn(h1b_s, h1_s, w_fi_b, w_fo_b, g2, b2, bs, tf)

    new_pool_p = u_p[n - POOL_STATE:][None, None]
    new_pool_s = jnp.concatenate([state_t[1:], u_s[None]], axis=0).transpose(1, 0, 2)[None]
    k_out = kT_p.reshape(*hd, n).transpose(2, 0, 1)[None, None]
    v_out = vT_p.reshape(*hd, n).transpose(2, 0, 1)[None, None]
    return (y_p[None], y_s[:, None], k_out, v_out,
            k_s.reshape(1, bs, 1, *hd), v_s.reshape(1, bs, 1, *hd),
            new_pool_p, new_pool_s)
```

```python
import functools
import math

import numpy as np
import jax
import jax.numpy as jnp
from jax import lax
from jax.experimental import pallas as pl
from jax.experimental.pallas import tpu as pltpu

D_MODEL = 2048
N_HEADS = 16
HEAD_DIM = 64
D_ATTN = N_HEADS * HEAD_DIM
MOBA_BLOCK = 256
MOBA_TOPK = 3
POOL_WINDOWS = (2, 4, 8, 16)
N_POOL_GROUPS = len(POOL_WINDOWS)
POOL_GROUP_DIM = 256
D_POOL = N_POOL_GROUPS * POOL_GROUP_DIM
POOL_OUT_GROUP = D_MODEL // N_POOL_GROUPS
POOL_STATE = max(POOL_WINDOWS) - 1
POOL_HALO = 16
N_BUCKETS = 32
MAX_DISTANCE = 128
D_FF = -(-8 * D_MODEL // (3 * 256)) * 256
ALPHA = 2.0 ** 0.25
LN_EPS = 1e-5
NEG_BIG = -1e30
POS_BIG = 1e30
LANES = 128
V_ROWS = 80
ATTN_UNROLL = 4
ATTN_HEADS = 4
MAX_SHIFT_EXCESS = 60.0
LOG2E = 1.4426950408889634
SCAN_BLOCKS = 8
FFN_SUB_ROWS = 512
VMEM_LIMIT = 56 * 1024 * 1024

F32 = jnp.float32
BF16 = jnp.bfloat16


def _t5_bucket_np(rel):
    n = np.maximum(rel, 0)
    max_exact = N_BUCKETS // 2
    nf = np.maximum(n, 1).astype(np.float32)
    large = max_exact + (np.log(nf / max_exact) / math.log(MAX_DISTANCE / max_exact)
                         * (N_BUCKETS - max_exact)).astype(np.int32)
    large = np.minimum(large, N_BUCKETS - 1)
    return np.where(n < max_exact, n, large)


_BUCKETS = _t5_bucket_np(np.arange(0, 4 * MOBA_BLOCK))
BUCKET_START = tuple(int(np.argmax(_BUCKETS >= t)) for t in range(N_BUCKETS))
FAR_REL = BUCKET_START[N_BUCKETS - 1]
assert FAR_REL <= MOBA_BLOCK


def _params(*sem):
    return pltpu.CompilerParams(dimension_semantics=sem, vmem_limit_bytes=VMEM_LIMIT)


def _layer_norm(y, g, b):
    mu = jnp.mean(y, axis=-1, keepdims=True)
    d = y - mu
    var = jnp.mean(d * d, axis=-1, keepdims=True)
    return d * lax.rsqrt(var + LN_EPS) * g + b


def _bias_tile(relb_ref, h, rel):
    b = jnp.full(rel.shape, relb_ref[0, h], F32)
    for t in range(1, N_BUCKETS):
        b = jnp.where(rel >= BUCKET_START[t], relb_ref[t, h], b)
    return b


def _top_blocks(score, idx, n_idx, axis):
    chosen = jnp.zeros(score.shape, F32)
    firsts = []
    left = score
    for _ in range(MOBA_TOPK):
        mx = jnp.max(left, axis=axis, keepdims=True)
        first = jnp.min(jnp.where(left == mx, idx, float(n_idx)), axis=axis, keepdims=True)
        hit = idx == first
        chosen = jnp.where(hit, 1.0, chosen)
        left = jnp.where(hit, -jnp.inf, left)
        firsts.append(first)
    return chosen, firsts


def _proj_kernel(x_ref, w_ref, *out_refs, epilogue):
    z = jnp.dot(x_ref[...], w_ref[...], preferred_element_type=F32)
    epilogue(z, *out_refs)


def _proj(x, wb, col0, n_cols, tm, tn, outs, epilogue):
    m, k = x.shape
    assert m % tm == 0 and n_cols % tn == 0 and col0 % tn == 0
    return pl.pallas_call(
        functools.partial(_proj_kernel, epilogue=epilogue),
        grid=(n_cols // tn, m // tm),
        in_specs=[pl.BlockSpec((tm, k), lambda j, i: (i, 0)),
                  pl.BlockSpec((k, tn), lambda j, i: (0, col0 // tn + j))],
        out_specs=[o[1] for o in outs],
        out_shape=[o[0] for o in outs],
        compiler_params=_params("arbitrary", "arbitrary"),
    )(x, wb)


def _rowmajor_out(m, n_cols, tm, tn, dtype):
    return (jax.ShapeDtypeStruct((m, n_cols), dtype), pl.BlockSpec((tm, tn), lambda j, i: (i, j)))


def _transposed_out(m, n_cols, tm, tn, dtype):
    return (jax.ShapeDtypeStruct((n_cols, m), dtype), pl.BlockSpec((tn, tm), lambda j, i: (j, i)))


def _ep_f32(z, o_ref):
    o_ref[...] = z


def _ep_q_scaled(z, o_ref):
    o_ref[...] = z * (1.0 / math.sqrt(HEAD_DIM))


def _ep_qT(z, oT_ref):
    oT_ref[...] = (z * (1.0 / math.sqrt(HEAD_DIM))).T.astype(BF16)


def _ep_k(z, kT_ref, kb_ref, km_ref):
    tm, tn = z.shape
    kT_ref[...] = z.T
    zb = z.astype(BF16)
    for h in range(tn // HEAD_DIM):
        kb_ref[h] = zb[:, h * HEAD_DIM:(h + 1) * HEAD_DIM]
    km_ref[0] = jnp.mean(z.reshape(tm // MOBA_BLOCK, MOBA_BLOCK, tn), axis=1)


def _ep_v(z, vT_ref, vb_ref):
    tm, tn = z.shape
    zT = z.T
    vT_ref[...] = zT
    zTb = zT.astype(BF16)
    pad_rows = V_ROWS - HEAD_DIM
    ones_row = jnp.where(lax.broadcasted_iota(jnp.int32, (pad_rows, MOBA_BLOCK), 0) == 0,
                         1.0, 0.0).astype(BF16)
    for h in range(tn // HEAD_DIM):
        for jb in range(tm // MOBA_BLOCK):
            vb_ref[h, jb, 0:HEAD_DIM, :] = zTb[h * HEAD_DIM:(h + 1) * HEAD_DIM,
                                               jb * MOBA_BLOCK:(jb + 1) * MOBA_BLOCK]
            vb_ref[h, jb, HEAD_DIM:V_ROWS, :] = ones_row


def _ep_gate(z, o_ref):
    o_ref[...] = jax.nn.sigmoid(z).astype(BF16)


def _pool_prompt_kernel(u_ref, halo_ref, o_ref):
    i = pl.program_id(0)
    tm = u_ref.shape[0]
    u = u_ref[...]
    halo = jnp.where(i > 0, halo_ref[...], 0.0)
    ext = jnp.concatenate([halo, u], axis=0)
    pos = i * tm + lax.broadcasted_iota(jnp.int32, (tm, 1), 0)
    for g, w in enumerate(POOL_WINDOWS):
        cols = slice(g * POOL_GROUP_DIM, (g + 1) * POOL_GROUP_DIM)
        s = ext[:, cols]
        span = 1
        while span < w:
            s = s + pltpu.roll(s, span, axis=0)
            span *= 2
        cnt = jnp.minimum(pos + 1, w).astype(F32)
        o_ref[:, cols] = (s[POOL_HALO:] / cnt - u[:, cols]).astype(o_ref.dtype)


def _pool_prompt(u, tm):
    m = u.shape[0]
    hb = tm // POOL_HALO
    return pl.pallas_call(
        _pool_prompt_kernel,
        grid=(m // tm,),
        in_specs=[pl.BlockSpec((tm, D_POOL), lambda i: (i, 0)),
                  pl.BlockSpec((POOL_HALO, D_POOL), lambda i: (jnp.maximum(i * hb - 1, 0), 0))],
        out_specs=pl.BlockSpec((tm, D_POOL), lambda i: (i, 0)),
        out_shape=jax.ShapeDtypeStruct((m, D_POOL), BF16),
        compiler_params=_params("arbitrary"),
    )(u, u)


def _pool_sample_kernel(st_ref, u_ref, o_ref):
    u = u_ref[...]
    for g, w in enumerate(POOL_WINDOWS):
        cols = slice(g * POOL_GROUP_DIM, (g + 1) * POOL_GROUP_DIM)
        win = u[:, cols]
        for t in range(POOL_STATE - (w - 1), POOL_STATE):
            win = win + st_ref[t, :, cols]
        o_ref[:, cols] = (win / float(w) - u[:, cols]).astype(o_ref.dtype)


def _pool_sample(state_t, u):
    b = u.shape[0]
    return pl.pallas_call(
        _pool_sample_kernel,
        out_shape=jax.ShapeDtypeStruct((b, D_POOL), BF16),
        compiler_params=pltpu.CompilerParams(vmem_limit_bytes=VMEM_LIMIT),
    )(state_t, u)


def _mix_kernel(pooled_ref, a_ref, gp_ref, ga_ref, x_ref, wpool_ref, pscale_ref, wao_ref,
                wo_ref, g_ref, b_ref, h_ref, hb_ref, *, a_transposed):
    parts = [jnp.dot(pooled_ref[:, g * POOL_GROUP_DIM:(g + 1) * POOL_GROUP_DIM], wpool_ref[g],
                     preferred_element_type=F32) for g in range(N_POOL_GROUPS)]
    p_br = jnp.concatenate(parts, axis=-1) * pscale_ref[...]
    if a_transposed:
        a_br = lax.dot_general(a_ref[...], wao_ref[...], (((0,), (0,)), ((), ())),
                               preferred_element_type=F32)
    else:
        a_br = jnp.dot(a_ref[...], wao_ref[...], preferred_element_type=F32)
    merged = gp_ref[...].astype(F32) * p_br + ga_ref[...].astype(F32) * a_br
    mix = jnp.dot(merged.astype(BF16), wo_ref[...], preferred_element_type=F32)
    h = _layer_norm(ALPHA * x_ref[...] + mix, g_ref[...], b_ref[...])
    h_ref[...] = h
    hb_ref[...] = h.astype(BF16)


def _mix(pooled, a, gates, x, wpool, pscale, wao, wo, ln_g, ln_b, tm, a_transposed):
    m = x.shape[0]
    row = lambda i: (i, 0)
    const2 = lambda i: (0, 0)
    a_spec = (pl.BlockSpec((D_ATTN, tm), lambda i: (0, i)) if a_transposed
              else pl.BlockSpec((tm, D_ATTN), row))
    return pl.pallas_call(
        functools.partial(_mix_kernel, a_transposed=a_transposed),
        grid=(m // tm,),
        in_specs=[pl.BlockSpec((tm, D_POOL), row),
                  a_spec,
                  pl.BlockSpec((tm, D_MODEL), lambda i: (i, 0)),
                  pl.BlockSpec((tm, D_MODEL), lambda i: (i, 1)),
                  pl.BlockSpec((tm, D_MODEL), row),
                  pl.BlockSpec((N_POOL_GROUPS, POOL_GROUP_DIM, POOL_OUT_GROUP), lambda i: (0, 0, 0)),
                  pl.BlockSpec((1, D_MODEL), const2),
                  pl.BlockSpec((D_ATTN, D_MODEL), const2),
                  pl.BlockSpec((D_MODEL, D_MODEL), const2),
                  pl.BlockSpec((1, D_MODEL), const2),
                  pl.BlockSpec((1, D_MODEL), const2)],
        out_specs=[pl.BlockSpec((tm, D_MODEL), row), pl.BlockSpec((tm, D_MODEL), row)],
        out_shape=[jax.ShapeDtypeStruct((m, D_MODEL), F32), jax.ShapeDtypeStruct((m, D_MODEL), BF16)],
        compiler_params=_params("arbitrary"),
    )(pooled, a, gates, gates, x, wpool, pscale, wao, wo, ln_g, ln_b)


def _ffn_kernel(hb_ref, h_hbm, wg_ref, wu_ref, wo_ref, g_ref, b_ref, o_ref, hres_ref, sem, *, sub_rows):
    i = pl.program_id(0)
    f = pl.program_id(1)
    tm = o_ref.shape[0]
    residual_copy = pltpu.make_async_copy(h_hbm.at[pl.ds(i * tm, tm)], hres_ref, sem)

    @pl.when(f == 0)
    def _():
        residual_copy.start()
        o_ref[...] = jnp.zeros_like(o_ref)

    for r in range(tm // sub_rows):
        rows = pl.ds(r * sub_rows, sub_rows)
        hb = hb_ref[rows, :]
        gate = jnp.dot(hb, wg_ref[...], preferred_element_type=F32)
        up = jnp.dot(hb, wu_ref[...], preferred_element_type=F32)
        act = (gate * jax.nn.sigmoid(gate) * up).astype(BF16)
        o_ref[rows, :] += jnp.dot(act, wo_ref[...], preferred_element_type=F32)

    @pl.when(f == pl.num_programs(1) - 1)
    def _():
        residual_copy.wait()
        o_ref[...] = _layer_norm(ALPHA * hres_ref[...] + o_ref[...], g_ref[...], b_ref[...])


def _ffn(hb, h, w_in, w_out, ln_g, ln_b, tm, tf):
    m = h.shape[0]
    nf = D_FF // tf
    sub_rows = math.gcd(tm, FFN_SUB_ROWS)
    row = lambda i, f: (i, 0)
    const2 = lambda i, f: (0, 0)
    return pl.pallas_call(
        functools.partial(_ffn_kernel, sub_rows=sub_rows),
        grid=(m // tm, nf),
        in_specs=[pl.BlockSpec((tm, D_MODEL), row),
                  pl.BlockSpec(memory_space=pl.ANY),
                  pl.BlockSpec((D_MODEL, tf), lambda i, f: (0, f)),
                  pl.BlockSpec((D_MODEL, tf), lambda i, f: (0, nf + f)),
                  pl.BlockSpec((tf, D_MODEL), lambda i, f: (f, 0)),
                  pl.BlockSpec((1, D_MODEL), const2),
                  pl.BlockSpec((1, D_MODEL), const2)],
        out_specs=pl.BlockSpec((tm, D_MODEL), row),
        out_shape=jax.ShapeDtypeStruct((m, D_MODEL), F32),
        scratch_shapes=[pltpu.VMEM((tm, D_MODEL), F32), pltpu.SemaphoreType.DMA(())],
        compiler_params=_params("arbitrary", "arbitrary"),
    )(hb, h, w_in, w_in, w_out, ln_g, ln_b)


def _attn_kernel(pt_ref, relb_ref, qT_ref, k_ref, vT_ref, km_ref, *refs):
    n_pages = 2 * SCAN_BLOCKS
    page_refs = refs[:n_pages]
    qrep_ref, oT_ref, l_ref, selo_ref, bias_ref, sel_ref, sc_ref = refs[n_pages:]
    scan_steps = pt_ref.shape[1] // n_pages
    step = pl.program_id(0) * pl.num_programs(1) + pl.program_id(1)

    c = pl.program_id(1)
    ng, nb = k_ref.shape[:2]
    heads = [pl.program_id(0) * ng + g for g in range(ng)]
    blk = MOBA_BLOCK
    prev, own = 0, 1

    @pl.when(c == 0)
    def _():
        kk = lax.broadcasted_iota(jnp.int32, (blk, blk), 0)
        qq = lax.broadcasted_iota(jnp.int32, (blk, blk), 1)
        rel = qq - kk
        for g, h in enumerate(heads):
            bias_ref[g, prev] = _bias_tile(relb_ref, h, rel + blk)
            bias_ref[g, own] = jnp.where(rel >= 0, _bias_tile(relb_ref, h, rel), NEG_BIG)
        sc_ref[...] = jnp.zeros_like(sc_ref)

    _cache_scan_step(step % scan_steps, scan_steps, page_refs, qrep_ref, l_ref, selo_ref, sc_ref)

    qTs = [qT_ref[g] for g in range(ng)]

    jidx = lax.broadcasted_iota(jnp.int32, (nb, blk), 0).astype(F32)
    cf = c.astype(F32)
    for g in range(ng):
        km = km_ref[g]
        km_hi = km.astype(BF16)
        r1 = km - km_hi.astype(F32)
        km_mid = r1.astype(BF16)
        km_lo = (r1 - km_mid.astype(F32)).astype(BF16)
        score = (jnp.dot(km_hi, qTs[g], preferred_element_type=F32)
                 + jnp.dot(km_mid, qTs[g], preferred_element_type=F32)
                 + jnp.dot(km_lo, qTs[g], preferred_element_type=F32))
        chosen, _ = _top_blocks(jnp.where(jidx < cf, score, -jnp.inf), jidx, nb, 0)
        sel_ref[g] = jnp.where(jidx < cf, chosen, jnp.where(jidx == cf, 1.0, 0.0))

    n_far = jnp.maximum(c - 1, 0)
    trips = (n_far + ATTN_UNROLL - 1) // ATTN_UNROLL
    far_bias = [relb_ref[N_BUCKETS - 1, h] * LOG2E for h in heads]
    jp = jnp.maximum(c - 1, 0)

    def col_max(s):
        return jnp.max(s, axis=0, keepdims=True)

    def visit(s, keep, v_tile, shift, acc, top):
        top = jnp.maximum(top, jnp.where(keep, col_max(s), NEG_BIG))
        p = jnp.exp2(s - jnp.where(keep, shift, POS_BIG))
        return acc + jnp.dot(v_tile, p.astype(BF16), preferred_element_type=F32), top

    def attend(shift):
        accs = [jnp.zeros((V_ROWS, blk), F32)] * ng
        tops = [jnp.full((1, blk), NEG_BIG, F32)] * ng
        near = [(c, own, None), (jp, prev, c > 0)]
        logits = [(jnp.dot(k_ref[g, j], qTs[g], preferred_element_type=F32) + bias_ref[g, kind]) * LOG2E
                  for j, kind, _ in near for g in range(ng)]
        for i, (j, _, allowed) in enumerate(near):
            for g in range(ng):
                keep = sel_ref[g, pl.ds(j, 1), :] > 0.5
                keep = keep if allowed is None else jnp.logical_and(keep, allowed)
                accs[g], tops[g] = visit(logits[i * ng + g], keep, vT_ref[g, j], shift[g], accs[g], tops[g])

        def body(t, carry):
            accs, tops = list(carry[0]), list(carry[1])
            tiles = [(t * ATTN_UNROLL + u, g) for u in range(ATTN_UNROLL) for g in range(ng)]
            logits = [jnp.dot(k_ref[g, j], qTs[g], preferred_element_type=F32) * LOG2E for j, g in tiles]
            for (j, g), s in zip(tiles, logits):
                keep = jnp.logical_and(sel_ref[g, pl.ds(j, 1), :] > 0.5, j < n_far)
                accs[g], top = visit(s, keep, vT_ref[g, j], shift[g] - far_bias[g], accs[g],
                                     tops[g] - far_bias[g])
                tops[g] = top + far_bias[g]
            return tuple(accs), tuple(tops)

        return lax.fori_loop(0, trips, body, (tuple(accs), tuple(tops)))

    def write_out(accs):
        for g in range(ng):
            oT_ref[g] = (accs[g][:HEAD_DIM] / accs[g][HEAD_DIM:HEAD_DIM + 1]).astype(oT_ref.dtype)

    accs, tops = attend([jnp.zeros((1, blk), F32)] * ng)
    write_out(accs)
    off = jnp.abs(tops[0])
    for g in range(1, ng):
        off = jnp.maximum(off, jnp.abs(tops[g]))

    @pl.when(jnp.max(off) > MAX_SHIFT_EXCESS)
    def _():
        accs, _ = attend(tops)
        write_out(accs)


def _attn_prompt_and_cache_scan(rel_bias, qT, k4, vT4, km3, page_table, cache_t, q_rep):
    nh, _, n = qT.shape
    nb = n // MOBA_BLOCK
    ng = ATTN_HEADS
    assert nb % ATTN_UNROLL == 0 and nh % ng == 0
    bs, pages_per_seq = page_table.shape
    page = cache_t.shape[-1]
    assert MOBA_BLOCK == 2 * page and page == LANES
    nbs = pages_per_seq // 2
    pps = 2 * SCAN_BLOCKS
    scan_steps = pages_per_seq // pps
    assert nbs % SCAN_BLOCKS == 0 and MOBA_TOPK <= SCAN_BLOCKS and nbs <= LANES
    assert (nh // ng) * nb == bs * scan_steps

    def scan_pos(h, c):
        step = h * nb + c
        return step // scan_steps, step % scan_steps

    def page_map(h, c, pt, i):
        seq, t = scan_pos(h, c)
        return pt[seq, t * pps + i], 0, 0, 0

    page_specs = [pl.BlockSpec((1, N_HEADS, HEAD_DIM, page), functools.partial(page_map, i=i))
                  for i in range(pps)]
    return pl.pallas_call(
        _attn_kernel,
        grid_spec=pltpu.PrefetchScalarGridSpec(
            num_scalar_prefetch=1,
            grid=(nh // ng, nb),
            in_specs=[pl.BlockSpec(memory_space=pltpu.SMEM),
                      pl.BlockSpec((ng, HEAD_DIM, MOBA_BLOCK), lambda h, c, pt: (h, 0, c)),
                      pl.BlockSpec((ng, nb, MOBA_BLOCK, HEAD_DIM), lambda h, c, pt: (h, 0, 0, 0)),
                      pl.BlockSpec((ng, nb, V_ROWS, MOBA_BLOCK), lambda h, c, pt: (h, 0, 0, 0)),
                      pl.BlockSpec((ng, nb, HEAD_DIM), lambda h, c, pt: (h, 0, 0))]
                     + page_specs
                     + [pl.BlockSpec((1, N_HEADS, HEAD_DIM, page),
                                     lambda h, c, pt: (scan_pos(h, c)[0], 0, 0, 0))],
            out_specs=[pl.BlockSpec((ng, HEAD_DIM, MOBA_BLOCK), lambda h, c, pt: (h, 0, c)),
                       pl.BlockSpec((1, 1, N_HEADS, SCAN_BLOCKS * MOBA_BLOCK),
                                    lambda h, c, pt: (*scan_pos(h, c), 0, 0)),
                       pl.BlockSpec((1, N_HEADS, LANES), lambda h, c, pt: (scan_pos(h, c)[0], 0, 0))],
            scratch_shapes=[pltpu.VMEM((ng, 2, MOBA_BLOCK, MOBA_BLOCK), F32),
                            pltpu.VMEM((ng, nb, MOBA_BLOCK), F32),
                            pltpu.VMEM((N_HEADS, LANES), F32)]),
        out_shape=[jax.ShapeDtypeStruct((nh, HEAD_DIM, n), BF16),
                   jax.ShapeDtypeStruct((bs, scan_steps, N_HEADS, SCAN_BLOCKS * MOBA_BLOCK), F32),
                   jax.ShapeDtypeStruct((bs, N_HEADS, LANES), jnp.int32)],
        compiler_params=_params("arbitrary", "arbitrary"),
    )(page_table, rel_bias, qT, k4, vT4, km3, *([cache_t] * pps), q_rep)


def _cache_scan_step(t, nt, page_refs, q_ref, l_ref, sel_ref, sc_ref):
    lane = lax.broadcasted_iota(jnp.int32, sc_ref.shape, 1)
    n_seen = (t + 1) * SCAN_BLOCKS
    page = page_refs[0].shape[-1]
    for h in range(q_ref.shape[1]):
        qh = q_ref[0, h]
        for i, page_ref in enumerate(page_refs):
            l_ref[0, 0, pl.ds(h, 1), i * page:(i + 1) * page] = jnp.sum(page_ref[0, h] * qh, axis=0,
                                                                        keepdims=True)
    sc = sc_ref[...]
    for jj in range(SCAN_BLOCKS):
        logit = l_ref[0, 0, :, jj * MOBA_BLOCK:(jj + 1) * MOBA_BLOCK]
        mean = jnp.sum(logit, axis=-1, keepdims=True) * (1.0 / MOBA_BLOCK)
        sc = jnp.where(lane == t * SCAN_BLOCKS + jj, mean, sc)
    sc_ref[...] = sc
    _, firsts = _top_blocks(jnp.where(lane < n_seen, sc, -jnp.inf), lane.astype(F32), LANES, 1)
    out = jnp.zeros(sc.shape, F32)
    for r, first in enumerate(firsts):
        out = jnp.where(lane == r, first, out)
    sel_ref[0] = out.astype(jnp.int32)


def _decode_attend_kernel(sel_ref, pt_ref, relb_ref, l_ref, relbt_ref, q_ref, kn_ref, vn_ref, cv_hbm,
                          a_ref, vbuf, sem, rows_ref, blast_ref):
    s = pl.program_id(0)
    ns = pl.num_programs(0)
    nh = l_ref.shape[2]
    nb = l_ref.shape[1] * SCAN_BLOCKS
    page = vbuf.shape[-1]
    n_tiles = 2 * MOBA_TOPK
    slot = s % 2

    def tile_copy(seq, h, t, buf_slot):
        blk = sel_ref[seq, h * MOBA_TOPK + t // 2]
        return pltpu.make_async_copy(cv_hbm.at[pt_ref[seq, 2 * blk + t % 2], h],
                                     vbuf.at[buf_slot, h, t], sem.at[buf_slot])

    def start_fetch(seq, buf_slot):
        def per_head(h, carry):
            for t in range(n_tiles):
                tile_copy(seq, h, t, buf_slot).start()
            return carry
        lax.fori_loop(0, nh, per_head, 0)

    @pl.when(s == 0)
    def _():
        start_fetch(0, 0)
        pos = lax.broadcasted_iota(jnp.int32, (1, MOBA_BLOCK), 1)
        for h in range(nh):
            blast_ref[pl.ds(h, 1), :] = _bias_tile(relb_ref, h, MOBA_BLOCK - pos)

    @pl.when(s + 1 < ns)
    def _():
        start_fetch(s + 1, 1 - slot)

    for h in range(nh):
        for r in range(MOBA_TOPK):
            blk = sel_ref[s, h * MOBA_TOPK + r]
            wide = l_ref[0, blk // SCAN_BLOCKS, pl.ds(h, 1), :]
            row = wide[:, :MOBA_BLOCK]
            for seg in range(1, SCAN_BLOCKS):
                row = jnp.where(blk % SCAN_BLOCKS == seg, wide[:, seg * MOBA_BLOCK:(seg + 1) * MOBA_BLOCK], row)
            bias =jnp.where(blk == nb - 1, blast_ref[pl.ds(h, 1), :], relb_ref[N_BUCKETS - 1, h])
            rows_ref[r, pl.ds(h, 1), :] = row + bias
    q = q_ref[0]
    s_own = jnp.sum(q * kn_ref[0], axis=-1, keepdims=True) + relbt_ref[:, 0:1]
    m = s_own
    for r in range(MOBA_TOPK):
        m = jnp.maximum(m, jnp.max(rows_ref[r], axis=-1, keepdims=True))
    w_own = jnp.exp(s_own - m)
    ps = [jnp.exp(rows_ref[r] - m) for r in range(MOBA_TOPK)]
    den = w_own
    for p in ps:
        den = den + jnp.sum(p, axis=-1, keepdims=True)
    p_all = jnp.concatenate(ps, axis=-1).astype(BF16)
    num_own = w_own * vn_ref[0]

    def wait_head(h, carry):
        for t in range(n_tiles):
            tile_copy(s, h, t, slot).wait()
        return carry
    lax.fori_loop(0, nh, wait_head, 0)

    for h in range(nh):
        vt = jnp.concatenate([vbuf[slot, h, t] for t in range(n_tiles)], axis=-1).astype(BF16)
        ph = jnp.broadcast_to(p_all[h:h + 1], (8, n_tiles * page))
        pv = lax.dot_general(ph, vt, (((1,), (1,)), ((), ())), preferred_element_type=F32)
        a_ref[0, pl.ds(h, 1), :] = (pv[0:1] + num_own[h:h + 1]) / den[h:h + 1]


def _decode_attend(sel2, page_table, rel_bias, logits, cache_vt, q3, kn3, vn3):
    b, n_groups, nh, width = logits.shape
    page = cache_vt.shape[-1]
    tok = pl.BlockSpec((1, nh, HEAD_DIM), lambda s, sel, pt: (s, 0, 0))
    return pl.pallas_call(
        _decode_attend_kernel,
        grid_spec=pltpu.PrefetchScalarGridSpec(
            num_scalar_prefetch=2,
            grid=(b,),
            in_specs=[pl.BlockSpec(memory_space=pltpu.SMEM),
                      pl.BlockSpec((1, n_groups, nh, width), lambda s, sel, pt: (s, 0, 0, 0)),
                      pl.BlockSpec((nh, N_BUCKETS), lambda s, sel, pt: (0, 0)),
                      tok, tok, tok,
                      pl.BlockSpec(memory_space=pl.ANY)],
            out_specs=tok,
            scratch_shapes=[pltpu.VMEM((2, nh, 2 * MOBA_TOPK, HEAD_DIM, page), F32),
                            pltpu.SemaphoreType.DMA((2,)),
                            pltpu.VMEM((MOBA_TOPK, nh, MOBA_BLOCK), F32),
                            pltpu.VMEM((nh, MOBA_BLOCK), F32)]),
        out_shape=jax.ShapeDtypeStruct((b, nh, HEAD_DIM), F32),
        compiler_params=_params("arbitrary"),
    )(sel2, page_table, rel_bias, logits, rel_bias.T, q3, kn3, vn3, cache_vt)


def kernel(x_prompt, x_sample, cache_k, cache_v, state_pool, page_table, rel_bias, w_in, w_pool,
           pool_scale, w_attn_out, w_o, ln1_g, ln1_b, w_ffn_in, w_ffn_out, ln2_g, ln2_b):
    assert w_in.shape[0] == 1 and x_prompt.shape[0] == 1 and x_sample.shape[1] == 1
    n = x_prompt.shape[1]
    bs = x_sample.shape[0]
    nb = n // MOBA_BLOCK
    page = cache_k.shape[2]
    past_len = page_table.shape[1] * page
    assert past_len % MOBA_BLOCK == 0 and past_len >= max(POOL_WINDOWS)

    w_in_b = w_in[0].astype(BF16)
    w_pool_b = w_pool[0].astype(BF16)
    w_ao_b = w_attn_out[0].astype(BF16)
    w_o_b = w_o[0].astype(BF16)
    tf = 512
    w_fi_b = w_ffn_in[0].astype(BF16)
    w_fo_b = w_ffn_out[0].astype(BF16)
    pscale = pool_scale[0][None]
    g1, b1, g2, b2 = ln1_g[0][None], ln1_b[0][None], ln2_g[0][None], ln2_b[0][None]

    xp = x_prompt[0]
    xs = x_sample[:, 0]
    xp_b = xp.astype(BF16)
    xs_b = xs.astype(BF16)
    col_q, col_k, col_v, col_g = D_POOL, D_POOL + D_ATTN, D_POOL + 2 * D_ATTN, D_POOL + 3 * D_ATTN
    hd = (N_HEADS, HEAD_DIM)

    tm, tn = 1024, 1024
    (u_p,) = _proj(xp_b, w_in_b, 0, D_POOL, tm, tn, [_rowmajor_out(n, D_POOL, tm, tn, F32)], _ep_f32)
    (qT_p,) = _proj(xp_b, w_in_b, col_q, D_ATTN, tm, tn,
                    [_transposed_out(n, D_ATTN, tm, tn, BF16)], _ep_qT)
    kT_p, kb_p, km_p = _proj(
        xp_b, w_in_b, col_k, D_ATTN, tm, tn,
        [_transposed_out(n, D_ATTN, tm, tn, F32),
         (jax.ShapeDtypeStruct((N_HEADS, n, HEAD_DIM), BF16),
          pl.BlockSpec((N_HEADS, tm, HEAD_DIM), lambda j, i: (j, i, 0))),
         (jax.ShapeDtypeStruct((n // tm, tm // MOBA_BLOCK, D_ATTN), F32),
          pl.BlockSpec((1, tm // MOBA_BLOCK, tn), lambda j, i: (i, 0, j)))],
        _ep_k)
    vT_p, vb_p = _proj(
        xp_b, w_in_b, col_v, D_ATTN, tm, tn,
        [_transposed_out(n, D_ATTN, tm, tn, F32),
         (jax.ShapeDtypeStruct((N_HEADS, nb, V_ROWS, MOBA_BLOCK), BF16),
          pl.BlockSpec((N_HEADS, tm // MOBA_BLOCK, V_ROWS, MOBA_BLOCK), lambda j, i: (j, i, 0, 0)))],
        _ep_v)
    (gates_p,) = _proj(xp_b, w_in_b, col_g, 2 * D_MODEL, tm, tn,
                       [_rowmajor_out(n, 2 * D_MODEL, tm, tn, BF16)], _ep_gate)

    row32 = lambda nc, dt: [_rowmajor_out(bs, nc, bs, tn, dt)]
    (u_s,) = _proj(xs_b, w_in_b, 0, D_POOL, bs, tn, row32(D_POOL, F32), _ep_f32)
    (q_s,) = _proj(xs_b, w_in_b, col_q, D_ATTN, bs, tn, row32(D_ATTN, F32), _ep_q_scaled)
    (k_s,) = _proj(xs_b, w_in_b, col_k, D_ATTN, bs, tn, row32(D_ATTN, F32), _ep_f32)
    (v_s,) = _proj(xs_b, w_in_b, col_v, D_ATTN, bs, tn, row32(D_ATTN, F32), _ep_f32)
    (gates_s,) = _proj(xs_b, w_in_b, col_g, 2 * D_MODEL, bs, tn, row32(2 * D_MODEL, BF16), _ep_gate)

    ck_t = cache_k[0].transpose(0, 2, 3, 1)
    cv_t = cache_v[0].transpose(0, 2, 3, 1)
    q_rep = jnp.broadcast_to(q_s.reshape(bs, *hd, 1), (bs, *hd, page))
    km3 = km_p.reshape(nb, N_HEADS, HEAD_DIM).transpose(1, 0, 2)
    aT, logits_s, sel_s = _attn_prompt_and_cache_scan(
        rel_bias, qT_p.reshape(N_HEADS, HEAD_DIM, n), kb_p.reshape(N_HEADS, nb, MOBA_BLOCK, HEAD_DIM),
        vb_p, km3, page_table, ck_t, q_rep)

    pooled_p = _pool_prompt(u_p, 512)
    h1_p, h1b_p = _mix(pooled_p, aT.reshape(D_ATTN, n), gates_p, xp, w_pool_b, pscale, w_ao_b, w_o_b,
                       g1, b1, 256, True)
    y_p = _ffn(h1b_p, h1_p, w_fi_b, w_fo_b, g2, b2, 1024, tf)

    sel2 = sel_s[:, :, :MOBA_TOPK].reshape(bs, N_HEADS * MOBA_TOPK)
    a_s = _decode_attend(sel2, page_table, rel_bias, logits_s, cv_t,
                         q_s.reshape(bs, *hd), k_s.reshape(bs, *hd), v_s.reshape(bs, *hd))
    a_s = a_s.reshape(bs, D_ATTN).astype(BF16)

    state_t = state_pool[0].transpose(1, 0, 2)
    pooled_s = _pool_sample(state_t, u_s)
    h1_s, h1b_s = _mix(pooled_s, a_s, gates_s, xs, w_pool_b, pscale, w_ao_b, w_o_b, g1, b1, bs, False)
    y_s = _ffn(h1b_s, h1_s, w_fi_b, w_fo_b, g2, b2, bs, tf)

    new_pool_p = u_p[n - POOL_STATE:][None, None]
    new_pool_s = jnp.concatenate([state_t[1:], u_s[None]], axis=0).transpose(1, 0, 2)[None]
    k_out = kT_p.reshape(*hd, n).transpose(2, 0, 1)[None, None]
    v_out = vT_p.reshape(*hd, n).transpose(2, 0, 1)[None, None]
    return (y_p[None], y_s[:, None], k_out, v_out,
            k_s.reshape(1, bs, 1, *hd), v_s.reshape(1, bs, 1, *hd),
            new_pool_p, new_pool_s)
```

```python
import functools
import math

import numpy as np
import jax
import jax.numpy as jnp
from jax import lax
from jax.experimental import pallas as pl
from jax.experimental.pallas import tpu as pltpu

D_MODEL = 2048
N_HEADS = 16
HEAD_DIM = 64
D_ATTN = N_HEADS * HEAD_DIM
MOBA_BLOCK = 256
MOBA_TOPK = 3
POOL_WINDOWS = (2, 4, 8, 16)
N_POOL_GROUPS = len(POOL_WINDOWS)
POOL_GROUP_DIM = 256
D_POOL = N_POOL_GROUPS * POOL_GROUP_DIM
POOL_OUT_GROUP = D_MODEL // N_POOL_GROUPS
POOL_STATE = max(POOL_WINDOWS) - 1
POOL_HALO = 16
N_BUCKETS = 32
MAX_DISTANCE = 128
D_FF = -(-8 * D_MODEL // (3 * 256)) * 256
ALPHA = 2.0 ** 0.25
LN_EPS = 1e-5
NEG_BIG = -1e30
POS_BIG = 1e30
LANES = 128
V_ROWS = 80
ATTN_UNROLL = 4
ATTN_HEADS = 4
MAX_SHIFT_EXCESS = 60.0
LOG2E = 1.4426950408889634
SCAN_BLOCKS = 8
FFN_SUB_ROWS = 512
VMEM_LIMIT = 56 * 1024 * 1024

F32 = jnp.float32
BF16 = jnp.bfloat16


def _t5_bucket_np(rel):
    n = np.maximum(rel, 0)
    max_exact = N_BUCKETS // 2
    nf = np.maximum(n, 1).astype(np.float32)
    large = max_exact + (np.log(nf / max_exact) / math.log(MAX_DISTANCE / max_exact)
                         * (N_BUCKETS - max_exact)).astype(np.int32)
    large = np.minimum(large, N_BUCKETS - 1)
    return np.where(n < max_exact, n, large)


_BUCKETS = _t5_bucket_np(np.arange(0, 4 * MOBA_BLOCK))
BUCKET_START = tuple(int(np.argmax(_BUCKETS >= t)) for t in range(N_BUCKETS))
FAR_REL = BUCKET_START[N_BUCKETS - 1]
assert FAR_REL <= MOBA_BLOCK


def _params(*sem):
    return pltpu.CompilerParams(dimension_semantics=sem, vmem_limit_bytes=VMEM_LIMIT)


def _layer_norm(y, g, b):
    mu = jnp.mean(y, axis=-1, keepdims=True)
    d = y - mu
    var = jnp.mean(d * d, axis=-1, keepdims=True)
    return d * lax.rsqrt(var + LN_EPS) * g + b


def _bias_tile(relb_ref, h, rel):
    b = jnp.full(rel.shape, relb_ref[0, h], F32)
    for t in range(1, N_BUCKETS):
        b = jnp.where(rel >= BUCKET_START[t], relb_ref[t, h], b)
    return b


def _top_blocks(score, idx, n_idx, axis):
    chosen = jnp.zeros(score.shape, F32)
    firsts = []
    left = score
    for _ in range(MOBA_TOPK):
        mx = jnp.max(left, axis=axis, keepdims=True)
        first = jnp.min(jnp.where(left == mx, idx, float(n_idx)), axis=axis, keepdims=True)
        hit = idx == first
        chosen = jnp.where(hit, 1.0, chosen)
        left = jnp.where(hit, -jnp.inf, left)
        firsts.append(first)
    return chosen, firsts


def _proj_kernel(x_ref, w_ref, *out_refs, epilogue):
    z = jnp.dot(x_ref[...], w_ref[...], preferred_element_type=F32)
    epilogue(z, *out_refs)


def _proj(x, wb, col0, n_cols, tm, tn, outs, epilogue):
    m, k = x.shape
    assert m % tm == 0 and n_cols % tn == 0 and col0 % tn == 0
    return pl.pallas_call(
        functools.partial(_proj_kernel, epilogue=epilogue),
        grid=(n_cols // tn, m // tm),
        in_specs=[pl.BlockSpec((tm, k), lambda j, i: (i, 0)),
                  pl.BlockSpec((k, tn), lambda j, i: (0, col0 // tn + j))],
        out_specs=[o[1] for o in outs],
        out_shape=[o[0] for o in outs],
        compiler_params=_params("arbitrary", "arbitrary"),
    )(x, wb)


def _rowmajor_out(m, n_cols, tm, tn, dtype):
    return (jax.ShapeDtypeStruct((m, n_cols), dtype), pl.BlockSpec((tm, tn), lambda j, i: (i, j)))


def _transposed_out(m, n_cols, tm, tn, dtype):
    return (jax.ShapeDtypeStruct((n_cols, m), dtype), pl.BlockSpec((tn, tm), lambda j, i: (j, i)))


def _ep_f32(z, o_ref):
    o_ref[...] = z


def _ep_q_scaled(z, o_ref):
    o_ref[...] = z * (1.0 / math.sqrt(HEAD_DIM))


def _ep_qT(z, oT_ref):
    oT_ref[...] = (z * (1.0 / math.sqrt(HEAD_DIM))).T.astype(BF16)


def _ep_k(z, kT_ref, kb_ref, km_ref):
    tm, tn = z.shape
    kT_ref[...] = z.T
    zb = z.astype(BF16)
    for h in range(tn // HEAD_DIM):
        kb_ref[h] = zb[:, h * HEAD_DIM:(h + 1) * HEAD_DIM]
    km_ref[0] = jnp.mean(z.reshape(tm // MOBA_BLOCK, MOBA_BLOCK, tn), axis=1)


def _ep_v(z, vT_ref, vb_ref):
    tm, tn = z.shape
    zT = z.T
    vT_ref[...] = zT
    zTb = zT.astype(BF16)
    pad_rows = V_ROWS - HEAD_DIM
    ones_row = jnp.where(lax.broadcasted_iota(jnp.int32, (pad_rows, MOBA_BLOCK), 0) == 0,
                         1.0, 0.0).astype(BF16)
    for h in range(tn // HEAD_DIM):
        for jb in range(tm // MOBA_BLOCK):
            vb_ref[h, jb, 0:HEAD_DIM, :] = zTb[h * HEAD_DIM:(h + 1) * HEAD_DIM,
                                               jb * MOBA_BLOCK:(jb + 1) * MOBA_BLOCK]
            vb_ref[h, jb, HEAD_DIM:V_ROWS, :] = ones_row


def _ep_gate(z, o_ref):
    o_ref[...] = jax.nn.sigmoid(z).astype(BF16)


def _pool_prompt_kernel(u_ref, halo_ref, o_ref):
    i = pl.program_id(0)
    tm = u_ref.shape[0]
    u = u_ref[...]
    halo = jnp.where(i > 0, halo_ref[...], 0.0)
    ext = jnp.concatenate([halo, u], axis=0)
    pos = i * tm + lax.broadcasted_iota(jnp.int32, (tm, 1), 0)
    for g, w in enumerate(POOL_WINDOWS):
        cols = slice(g * POOL_GROUP_DIM, (g + 1) * POOL_GROUP_DIM)
        s = ext[:, cols]
        span = 1
        while span < w:
            s = s + pltpu.roll(s, span, axis=0)
            span *= 2
        cnt = jnp.minimum(pos + 1, w).astype(F32)
        o_ref[:, cols] = (s[POOL_HALO:] / cnt - u[:, cols]).astype(o_ref.dtype)


def _pool_prompt(u, tm):
    m = u.shape[0]
    hb = tm // POOL_HALO
    return pl.pallas_call(
        _pool_prompt_kernel,
        grid=(m // tm,),
        in_specs=[pl.BlockSpec((tm, D_POOL), lambda i: (i, 0)),
                  pl.BlockSpec((POOL_HALO, D_POOL), lambda i: (jnp.maximum(i * hb - 1, 0), 0))],
        out_specs=pl.BlockSpec((tm, D_POOL), lambda i: (i, 0)),
        out_shape=jax.ShapeDtypeStruct((m, D_POOL), BF16),
        compiler_params=_params("arbitrary"),
    )(u, u)


def _pool_sample_kernel(st_ref, u_ref, o_ref):
    u = u_ref[...]
    for g, w in enumerate(POOL_WINDOWS):
        cols = slice(g * POOL_GROUP_DIM, (g + 1) * POOL_GROUP_DIM)
        win = u[:, cols]
        for t in range(POOL_STATE - (w - 1), POOL_STATE):
            win = win + st_ref[t, :, cols]
        o_ref[:, cols] = (win / float(w) - u[:, cols]).astype(o_ref.dtype)


def _pool_sample(state_t, u):
    b = u.shape[0]
    return pl.pallas_call(
        _pool_sample_kernel,
        out_shape=jax.ShapeDtypeStruct((b, D_POOL), BF16),
        compiler_params=pltpu.CompilerParams(vmem_limit_bytes=VMEM_LIMIT),
    )(state_t, u)


def _mix_kernel(pooled_ref, a_ref, gp_ref, ga_ref, x_ref, wpool_ref, pscale_ref, wao_ref,
                wo_ref, g_ref, b_ref, h_ref, hb_ref, *, a_transposed):
    parts = [jnp.dot(pooled_ref[:, g * POOL_GROUP_DIM:(g + 1) * POOL_GROUP_DIM], wpool_ref[g],
                     preferred_element_type=F32) for g in range(N_POOL_GROUPS)]
    p_br = jnp.concatenate(parts, axis=-1) * pscale_ref[...]
    if a_transposed:
        a_br = lax.dot_general(a_ref[...], wao_ref[...], (((0,), (0,)), ((), ())),
                               preferred_element_type=F32)
    else:
        a_br = jnp.dot(a_ref[...], wao_ref[...], preferred_element_type=F32)
    merged = gp_ref[...].astype(F32) * p_br + ga_ref[...].astype(F32) * a_br
    mix = jnp.dot(merged.astype(BF16), wo_ref[...], preferred_element_type=F32)
    h = _layer_norm(ALPHA * x_ref[...] + mix, g_ref[...], b_ref[...])
    h_ref[...] = h
    hb_ref[...] = h.astype(BF16)


def _mix(pooled, a, gates, x, wpool, pscale, wao, wo, ln_g, ln_b, tm, a_transposed):
    m = x.shape[0]
    row = lambda i: (i, 0)
    const2 = lambda i: (0, 0)
    a_spec = (pl.BlockSpec((D_ATTN, tm), lambda i: (0, i)) if a_transposed
              else pl.BlockSpec((tm, D_ATTN), row))
    return pl.pallas_call(
        functools.partial(_mix_kernel, a_transposed=a_transposed),
        grid=(m // tm,),
        in_specs=[pl.BlockSpec((tm, D_POOL), row),
                  a_spec,
                  pl.BlockSpec((tm, D_MODEL), lambda i: (i, 0)),
                  pl.BlockSpec((tm, D_MODEL), lambda i: (i, 1)),
                  pl.BlockSpec((tm, D_MODEL), row),
                  pl.BlockSpec((N_POOL_GROUPS, POOL_GROUP_DIM, POOL_OUT_GROUP), lambda i: (0, 0, 0)),
                  pl.BlockSpec((1, D_MODEL), const2),
                  pl.BlockSpec((D_ATTN, D_MODEL), const2),
                  pl.BlockSpec((D_MODEL, D_MODEL), const2),
                  pl.BlockSpec((1, D_MODEL), const2),
                  pl.BlockSpec((1, D_MODEL), const2)],
        out_specs=[pl.BlockSpec((tm, D_MODEL), row), pl.BlockSpec((tm, D_MODEL), row)],
        out_shape=[jax.ShapeDtypeStruct((m, D_MODEL), F32), jax.ShapeDtypeStruct((m, D_MODEL), BF16)],
        compiler_params=_params("arbitrary"),
    )(pooled, a, gates, gates, x, wpool, pscale, wao, wo, ln_g, ln_b)


def _swiglu_chunk(hb, wg_ref, wu_ref, wo_ref):
    gate = jnp.dot(hb, wg_ref[...], preferred_element_type=F32)
    up = jnp.dot(hb, wu_ref[...], preferred_element_type=F32)
    act = (gate * jax.nn.sigmoid(gate) * up).astype(BF16)
    return jnp.dot(act, wo_ref[...], preferred_element_type=F32)


def _ffn_kernel(hb_ref, h_hbm, hbs_ref, hs_ref, wg_ref, wu_ref, wo_ref, g_ref, b_ref, o_ref, os_ref,
                hres_ref, sem, *, sub_rows):
    i = pl.program_id(0)
    f = pl.program_id(1)
    last = pl.num_programs(1) - 1
    tm = o_ref.shape[0]
    residual_copy = pltpu.make_async_copy(h_hbm.at[pl.ds(i * tm, tm)], hres_ref, sem)

    @pl.when(f == 0)
    def _():
        residual_copy.start()
        o_ref[...] = jnp.zeros_like(o_ref)

    for r in range(tm // sub_rows):
        rows = pl.ds(r * sub_rows, sub_rows)
        o_ref[rows, :] += _swiglu_chunk(hb_ref[rows, :], wg_ref, wu_ref, wo_ref)

    @pl.when(f == last)
    def _():
        residual_copy.wait()
        o_ref[...] = _layer_norm(ALPHA * hres_ref[...] + o_ref[...], g_ref[...], b_ref[...])

    @pl.when(i == 0)
    def _():
        @pl.when(f == 0)
        def _():
            os_ref[...] = jnp.zeros_like(os_ref)

        os_ref[...] += _swiglu_chunk(hbs_ref[...], wg_ref, wu_ref, wo_ref)

        @pl.when(f == last)
        def _():
            os_ref[...] = _layer_norm(ALPHA * hs_ref[...] + os_ref[...], g_ref[...], b_ref[...])


def _ffn(hb, h, hb_s, h_s, w_in, w_out, ln_g, ln_b, tm, tf):
    m = h.shape[0]
    ms = h_s.shape[0]
    nf = D_FF // tf
    sub_rows = math.gcd(tm, FFN_SUB_ROWS)
    row = lambda i, f: (i, 0)
    const2 = lambda i, f: (0, 0)
    return pl.pallas_call(
        functools.partial(_ffn_kernel, sub_rows=sub_rows),
        grid=(m // tm, nf),
        in_specs=[pl.BlockSpec((tm, D_MODEL), row),
                  pl.BlockSpec(memory_space=pl.ANY),
                  pl.BlockSpec((ms, D_MODEL), const2),
                  pl.BlockSpec((ms, D_MODEL), const2),
                  pl.BlockSpec((D_MODEL, tf), lambda i, f: (0, f)),
                  pl.BlockSpec((D_MODEL, tf), lambda i, f: (0, nf + f)),
                  pl.BlockSpec((tf, D_MODEL), lambda i, f: (f, 0)),
                  pl.BlockSpec((1, D_MODEL), const2),
                  pl.BlockSpec((1, D_MODEL), const2)],
        out_specs=[pl.BlockSpec((tm, D_MODEL), row), pl.BlockSpec((ms, D_MODEL), const2)],
        out_shape=[jax.ShapeDtypeStruct((m, D_MODEL), F32), jax.ShapeDtypeStruct((ms, D_MODEL), F32)],
        scratch_shapes=[pltpu.VMEM((tm, D_MODEL), F32), pltpu.SemaphoreType.DMA(())],
        compiler_params=_params("arbitrary", "arbitrary"),
    )(hb, h, hb_s, h_s, w_in, w_in, w_out, ln_g, ln_b)


def _attn_kernel(pt_ref, relb_ref, qT_ref, k_ref, vT_ref, km_ref, *refs):
    n_pages = 2 * SCAN_BLOCKS
    page_refs = refs[:n_pages]
    qrep_ref, oT_ref, l_ref, selo_ref, bias_ref, sel_ref, sc_ref = refs[n_pages:]
    scan_steps = pt_ref.shape[1] // n_pages
    step = pl.program_id(0) * pl.num_programs(1) + pl.program_id(1)

    c = pl.program_id(1)
    ng, nb = k_ref.shape[:2]
    heads = [pl.program_id(0) * ng + g for g in range(ng)]
    blk = MOBA_BLOCK
    prev, own = 0, 1

    @pl.when(c == 0)
    def _():
        kk = lax.broadcasted_iota(jnp.int32, (blk, blk), 0)
        qq = lax.broadcasted_iota(jnp.int32, (blk, blk), 1)
        rel = qq - kk
        for g, h in enumerate(heads):
            bias_ref[g, prev] = _bias_tile(relb_ref, h, rel + blk)
            bias_ref[g, own] = jnp.where(rel >= 0, _bias_tile(relb_ref, h, rel), NEG_BIG)
        sc_ref[...] = jnp.zeros_like(sc_ref)

    _cache_scan_step(step % scan_steps, scan_steps, page_refs, qrep_ref, l_ref, selo_ref, sc_ref)

    qTs = [qT_ref[g] for g in range(ng)]

    jidx = lax.broadcasted_iota(jnp.int32, (nb, blk), 0).astype(F32)
    cf = c.astype(F32)
    for g in range(ng):
        km = km_ref[g]
        km_hi = km.astype(BF16)
        r1 = km - km_hi.astype(F32)
        km_mid = r1.astype(BF16)
        km_lo = (r1 - km_mid.astype(F32)).astype(BF16)
        score = (jnp.dot(km_hi, qTs[g], preferred_element_type=F32)
                 + jnp.dot(km_mid, qTs[g], preferred_element_type=F32)
                 + jnp.dot(km_lo, qTs[g], preferred_element_type=F32))
        chosen, _ = _top_blocks(jnp.where(jidx < cf, score, -jnp.inf), jidx, nb, 0)
        sel_ref[g] = jnp.where(jidx < cf, chosen, jnp.where(jidx == cf, 1.0, 0.0))

    n_far = jnp.maximum(c - 1, 0)
    trips = (n_far + ATTN_UNROLL - 1) // ATTN_UNROLL
    far_bias = [relb_ref[N_BUCKETS - 1, h] * LOG2E for h in heads]
    jp = jnp.maximum(c - 1, 0)

    def col_max(s):
        return jnp.max(s, axis=0, keepdims=True)

    def visit(s, keep, v_tile, shift, acc, top):
        top = jnp.maximum(top, jnp.where(keep, col_max(s), NEG_BIG))
        p = jnp.exp2(s - jnp.where(keep, shift, POS_BIG))
        return acc + jnp.dot(v_tile, p.astype(BF16), preferred_element_type=F32), top

    def attend(shift):
        accs = [jnp.zeros((V_ROWS, blk), F32)] * ng
        tops = [jnp.full((1, blk), NEG_BIG, F32)] * ng
        near = [(c, own, None), (jp, prev, c > 0)]
        logits = [(jnp.dot(k_ref[g, j], qTs[g], preferred_element_type=F32) + bias_ref[g, kind]) * LOG2E
                  for j, kind, _ in near for g in range(ng)]
        for i, (j, _, allowed) in enumerate(near):
            for g in range(ng):
                keep = sel_ref[g, pl.ds(j, 1), :] > 0.5
                keep = keep if allowed is None else jnp.logical_and(keep, allowed)
                accs[g], tops[g] = visit(logits[i * ng + g], keep, vT_ref[g, j], shift[g], accs[g], tops[g])

        def body(t, carry):
            accs, tops = list(carry[0]), list(carry[1])
            tiles = [(t * ATTN_UNROLL + u, g) for u in range(ATTN_UNROLL) for g in range(ng)]
            logits = [jnp.dot(k_ref[g, j], qTs[g], preferred_element_type=F32) * LOG2E for j, g in tiles]
            for (j, g), s in zip(tiles, logits):
                keep = jnp.logical_and(sel_ref[g, pl.ds(j, 1), :] > 0.5, j < n_far)
                accs[g], top = visit(s, keep, vT_ref[g, j], shift[g] - far_bias[g], accs[g],
                                     tops[g] - far_bias[g])
                tops[g] = top + far_bias[g]
            return tuple(accs), tuple(tops)

        return lax.fori_loop(0, trips, body, (tuple(accs), tuple(tops)))

    def write_out(accs):
        for g in range(ng):
            oT_ref[g] = (accs[g][:HEAD_DIM] / accs[g][HEAD_DIM:HEAD_DIM + 1]).astype(oT_ref.dtype)

    accs, tops = attend([jnp.zeros((1, blk), F32)] * ng)
    write_out(accs)
    off = jnp.abs(tops[0])
    for g in range(1, ng):
        off = jnp.maximum(off, jnp.abs(tops[g]))

    @pl.when(jnp.max(off) > MAX_SHIFT_EXCESS)
    def _():
        accs, _ = attend(tops)
        write_out(accs)


def _attn_prompt_and_cache_scan(rel_bias, qT, k4, vT4, km3, page_table, cache_t, q_rep):
    nh, _, n = qT.shape
    nb = n // MOBA_BLOCK
    ng = ATTN_HEADS
    assert nb % ATTN_UNROLL == 0 and nh % ng == 0
    bs, pages_per_seq = page_table.shape
    page = cache_t.shape[-1]
    assert MOBA_BLOCK == 2 * page and page == LANES
    nbs = pages_per_seq // 2
    pps = 2 * SCAN_BLOCKS
    scan_steps = pages_per_seq // pps
    assert nbs % SCAN_BLOCKS == 0 and MOBA_TOPK <= SCAN_BLOCKS and nbs <= LANES
    assert (nh // ng) * nb == bs * scan_steps

    def scan_pos(h, c):
        step = h * nb + c
        return step // scan_steps, step % scan_steps

    def page_map(h, c, pt, i):
        seq, t = scan_pos(h, c)
        return pt[seq, t * pps + i], 0, 0, 0

    page_specs = [pl.BlockSpec((1, N_HEADS, HEAD_DIM, page), functools.partial(page_map, i=i))
                  for i in range(pps)]
    return pl.pallas_call(
        _attn_kernel,
        grid_spec=pltpu.PrefetchScalarGridSpec(
            num_scalar_prefetch=1,
            grid=(nh // ng, nb),
            in_specs=[pl.BlockSpec(memory_space=pltpu.SMEM),
                      pl.BlockSpec((ng, HEAD_DIM, MOBA_BLOCK), lambda h, c, pt: (h, 0, c)),
                      pl.BlockSpec((ng, nb, MOBA_BLOCK, HEAD_DIM), lambda h, c, pt: (h, 0, 0, 0)),
                      pl.BlockSpec((ng, nb, V_ROWS, MOBA_BLOCK), lambda h, c, pt: (h, 0, 0, 0)),
                      pl.BlockSpec((ng, nb, HEAD_DIM), lambda h, c, pt: (h, 0, 0))]
                     + page_specs
                     + [pl.BlockSpec((1, N_HEADS, HEAD_DIM, page),
                                     lambda h, c, pt: (scan_pos(h, c)[0], 0, 0, 0))],
            out_specs=[pl.BlockSpec((ng, HEAD_DIM, MOBA_BLOCK), lambda h, c, pt: (h, 0, c)),
                       pl.BlockSpec((1, 1, N_HEADS, SCAN_BLOCKS * MOBA_BLOCK),
                                    lambda h, c, pt: (*scan_pos(h, c), 0, 0)),
                       pl.BlockSpec((1, N_HEADS, LANES), lambda h, c, pt: (scan_pos(h, c)[0], 0, 0))],
            scratch_shapes=[pltpu.VMEM((ng, 2, MOBA_BLOCK, MOBA_BLOCK), F32),
                            pltpu.VMEM((ng, nb, MOBA_BLOCK), F32),
                            pltpu.VMEM((N_HEADS, LANES), F32)]),
        out_shape=[jax.ShapeDtypeStruct((nh, HEAD_DIM, n), BF16),
                   jax.ShapeDtypeStruct((bs, scan_steps, N_HEADS, SCAN_BLOCKS * MOBA_BLOCK), F32),
                   jax.ShapeDtypeStruct((bs, N_HEADS, LANES), jnp.int32)],
        compiler_params=_params("arbitrary", "arbitrary"),
    )(page_table, rel_bias, qT, k4, vT4, km3, *([cache_t] * pps), q_rep)


def _cache_scan_step(t, nt, page_refs, q_ref, l_ref, sel_ref, sc_ref):
    lane = lax.broadcasted_iota(jnp.int32, sc_ref.shape, 1)
    n_seen = (t + 1) * SCAN_BLOCKS
    page = page_refs[0].shape[-1]
    for h in range(q_ref.shape[1]):
        qh = q_ref[0, h]
        for i, page_ref in enumerate(page_refs):
            l_ref[0, 0, pl.ds(h, 1), i * page:(i + 1) * page] = jnp.sum(page_ref[0, h] * qh, axis=0,
                                                                        keepdims=True)
    sc = sc_ref[...]
    for jj in range(SCAN_BLOCKS):
        logit = l_ref[0, 0, :, jj * MOBA_BLOCK:(jj + 1) * MOBA_BLOCK]
        mean = jnp.sum(logit, axis=-1, keepdims=True) * (1.0 / MOBA_BLOCK)
        sc = jnp.where(lane == t * SCAN_BLOCKS + jj, mean, sc)
    sc_ref[...] = sc
    _, firsts = _top_blocks(jnp.where(lane < n_seen, sc, -jnp.inf), lane.astype(F32), LANES, 1)
    out = jnp.zeros(sc.shape, F32)
    for r, first in enumerate(firsts):
        out = jnp.where(lane == r, first, out)
    sel_ref[0] = out.astype(jnp.int32)


def _decode_attend_kernel(sel_ref, pt_ref, relb_ref, l_ref, relbt_ref, q_ref, kn_ref, vn_ref, cv_hbm,
                          a_ref, vbuf, sem, rows_ref, blast_ref):
    s = pl.program_id(0)
    ns = pl.num_programs(0)
    nh = l_ref.shape[2]
    nb = l_ref.shape[1] * SCAN_BLOCKS
    page = vbuf.shape[-1]
    n_tiles = 2 * MOBA_TOPK
    slot = s % 2

    def tile_copy(seq, h, t, buf_slot):
        blk = sel_ref[seq, h * MOBA_TOPK + t // 2]
        return pltpu.make_async_copy(cv_hbm.at[pt_ref[seq, 2 * blk + t % 2], h],
                                     vbuf.at[buf_slot, h, t], sem.at[buf_slot])

    def start_fetch(seq, buf_slot):
        def per_head(h, carry):
            for t in range(n_tiles):
                tile_copy(seq, h, t, buf_slot).start()
            return carry
        lax.fori_loop(0, nh, per_head, 0)

    @pl.when(s == 0)
    def _():
        start_fetch(0, 0)
        pos = lax.broadcasted_iota(jnp.int32, (1, MOBA_BLOCK), 1)
        for h in range(nh):
            blast_ref[pl.ds(h, 1), :] = _bias_tile(relb_ref, h, MOBA_BLOCK - pos)

    @pl.when(s + 1 < ns)
    def _():
        start_fetch(s + 1, 1 - slot)

    for h in range(nh):
        for r in range(MOBA_TOPK):
            blk = sel_ref[s, h * MOBA_TOPK + r]
            wide = l_ref[0, blk // SCAN_BLOCKS, pl.ds(h, 1), :]
            row = wide[:, :MOBA_BLOCK]
            for seg in range(1, SCAN_BLOCKS):
                row = jnp.where(blk % SCAN_BLOCKS == seg, wide[:, seg * MOBA_BLOCK:(seg + 1) * MOBA_BLOCK], row)
            bias =jnp.where(blk == nb - 1, blast_ref[pl.ds(h, 1), :], relb_ref[N_BUCKETS - 1, h])
            rows_ref[r, pl.ds(h, 1), :] = row + bias
    q = q_ref[0]
    s_own = jnp.sum(q * kn_ref[0], axis=-1, keepdims=True) + relbt_ref[:, 0:1]
    m = s_own
    for r in range(MOBA_TOPK):
        m = jnp.maximum(m, jnp.max(rows_ref[r], axis=-1, keepdims=True))
    w_own = jnp.exp(s_own - m)
    ps = [jnp.exp(rows_ref[r] - m) for r in range(MOBA_TOPK)]
    den = w_own
    for p in ps:
        den = den + jnp.sum(p, axis=-1, keepdims=True)
    p_all = jnp.concatenate(ps, axis=-1).astype(BF16)
    num_own = w_own * vn_ref[0]

    def wait_head(h, carry):
        for t in range(n_tiles):
            tile_copy(s, h, t, slot).wait()
        return carry
    lax.fori_loop(0, nh, wait_head, 0)

    for h in range(nh):
        vt = jnp.concatenate([vbuf[slot, h, t] for t in range(n_tiles)], axis=-1).astype(BF16)
        ph = jnp.broadcast_to(p_all[h:h + 1], (8, n_tiles * page))
        pv = lax.dot_general(ph, vt, (((1,), (1,)), ((), ())), preferred_element_type=F32)
        a_ref[0, pl.ds(h, 1), :] = (pv[0:1] + num_own[h:h + 1]) / den[h:h + 1]


def _decode_attend(sel2, page_table, rel_bias, logits, cache_vt, q3, kn3, vn3):
    b, n_groups, nh, width = logits.shape
    page = cache_vt.shape[-1]
    tok = pl.BlockSpec((1, nh, HEAD_DIM), lambda s, sel, pt: (s, 0, 0))
    return pl.pallas_call(
        _decode_attend_kernel,
        grid_spec=pltpu.PrefetchScalarGridSpec(
            num_scalar_prefetch=2,
            grid=(b,),
            in_specs=[pl.BlockSpec(memory_space=pltpu.SMEM),
                      pl.BlockSpec((1, n_groups, nh, width), lambda s, sel, pt: (s, 0, 0, 0)),
                      pl.BlockSpec((nh, N_BUCKETS), lambda s, sel, pt: (0, 0)),
                      tok, tok, tok,
                      pl.BlockSpec(memory_space=pl.ANY)],
            out_specs=tok,
            scratch_shapes=[pltpu.VMEM((2, nh, 2 * MOBA_TOPK, HEAD_DIM, page), F32),
                            pltpu.SemaphoreType.DMA((2,)),
                            pltpu.VMEM((MOBA_TOPK, nh, MOBA_BLOCK), F32),
                            pltpu.VMEM((nh, MOBA_BLOCK), F32)]),
        out_shape=jax.ShapeDtypeStruct((b, nh, HEAD_DIM), F32),
        compiler_params=_params("arbitrary"),
    )(sel2, page_table, rel_bias, logits, rel_bias.T, q3, kn3, vn3, cache_vt)


def kernel(x_prompt, x_sample, cache_k, cache_v, state_pool, page_table, rel_bias, w_in, w_pool,
           pool_scale, w_attn_out, w_o, ln1_g, ln1_b, w_ffn_in, w_ffn_out, ln2_g, ln2_b):
    assert w_in.shape[0] == 1 and x_prompt.shape[0] == 1 and x_sample.shape[1] == 1
    n = x_prompt.shape[1]
    bs = x_sample.shape[0]
    nb = n // MOBA_BLOCK
    page = cache_k.shape[2]
    past_len = page_table.shape[1] * page
    assert past_len % MOBA_BLOCK == 0 and past_len >= max(POOL_WINDOWS)

    w_in_b = w_in[0].astype(BF16)
    w_pool_b = w_pool[0].astype(BF16)
    w_ao_b = w_attn_out[0].astype(BF16)
    w_o_b = w_o[0].astype(BF16)
    tf = 512
    w_fi_b = w_ffn_in[0].astype(BF16)
    w_fo_b = w_ffn_out[0].astype(BF16)
    pscale = pool_scale[0][None]
    g1, b1, g2, b2 = ln1_g[0][None], ln1_b[0][None], ln2_g[0][None], ln2_b[0][None]

    xp = x_prompt[0]
    xs = x_sample[:, 0]
    xp_b = xp.astype(BF16)
    xs_b = xs.astype(BF16)
    col_q, col_k, col_v, col_g = D_POOL, D_POOL + D_ATTN, D_POOL + 2 * D_ATTN, D_POOL + 3 * D_ATTN
    hd = (N_HEADS, HEAD_DIM)

    tm, tn = 1024, 1024
    (u_p,) = _proj(xp_b, w_in_b, 0, D_POOL, tm, tn, [_rowmajor_out(n, D_POOL, tm, tn, F32)], _ep_f32)
    (qT_p,) = _proj(xp_b, w_in_b, col_q, D_ATTN, tm, tn,
                    [_transposed_out(n, D_ATTN, tm, tn, BF16)], _ep_qT)
    kT_p, kb_p, km_p = _proj(
        xp_b, w_in_b, col_k, D_ATTN, tm, tn,
        [_transposed_out(n, D_ATTN, tm, tn, F32),
         (jax.ShapeDtypeStruct((N_HEADS, n, HEAD_DIM), BF16),
          pl.BlockSpec((N_HEADS, tm, HEAD_DIM), lambda j, i: (j, i, 0))),
         (jax.ShapeDtypeStruct((n // tm, tm // MOBA_BLOCK, D_ATTN), F32),
          pl.BlockSpec((1, tm // MOBA_BLOCK, tn), lambda j, i: (i, 0, j)))],
        _ep_k)
    vT_p, vb_p = _proj(
        xp_b, w_in_b, col_v, D_ATTN, tm, tn,
        [_transposed_out(n, D_ATTN, tm, tn, F32),
         (jax.ShapeDtypeStruct((N_HEADS, nb, V_ROWS, MOBA_BLOCK), BF16),
          pl.BlockSpec((N_HEADS, tm // MOBA_BLOCK, V_ROWS, MOBA_BLOCK), lambda j, i: (j, i, 0, 0)))],
        _ep_v)
    (gates_p,) = _proj(xp_b, w_in_b, col_g, 2 * D_MODEL, tm, tn,
                       [_rowmajor_out(n, 2 * D_MODEL, tm, tn, BF16)], _ep_gate)

    row32 = lambda nc, dt: [_rowmajor_out(bs, nc, bs, tn, dt)]
    (u_s,) = _proj(xs_b, w_in_b, 0, D_POOL, bs, tn, row32(D_POOL, F32), _ep_f32)
    (q_s,) = _proj(xs_b, w_in_b, col_q, D_ATTN, bs, tn, row32(D_ATTN, F32), _ep_q_scaled)
    (k_s,) = _proj(xs_b, w_in_b, col_k, D_ATTN, bs, tn, row32(D_ATTN, F32), _ep_f32)
    (v_s,) = _proj(xs_b, w_in_b, col_v, D_ATTN, bs, tn, row32(D_ATTN, F32), _ep_f32)
    (gates_s,) = _proj(xs_b, w_in_b, col_g, 2 * D_MODEL, bs, tn, row32(2 * D_MODEL, BF16), _ep_gate)

    ck_t = cache_k[0].transpose(0, 2, 3, 1)
    cv_t = cache_v[0].transpose(0, 2, 3, 1)
    q_rep = jnp.broadcast_to(q_s.reshape(bs, *hd, 1), (bs, *hd, page))
    km3 = km_p.reshape(nb, N_HEADS, HEAD_DIM).transpose(1, 0, 2)
    aT, logits_s, sel_s = _attn_prompt_and_cache_scan(
        rel_bias, qT_p.reshape(N_HEADS, HEAD_DIM, n), kb_p.reshape(N_HEADS, nb, MOBA_BLOCK, HEAD_DIM),
        vb_p, km3, page_table, ck_t, q_rep)

    pooled_p = _pool_prompt(u_p, 512)
    h1_p, h1b_p = _mix(pooled_p, aT.reshape(D_ATTN, n), gates_p, xp, w_pool_b, pscale, w_ao_b, w_o_b,
                       g1, b1, 256, True)

    sel2 = sel_s[:, :, :MOBA_TOPK].reshape(bs, N_HEADS * MOBA_TOPK)
    a_s = _decode_attend(sel2, page_table, rel_bias, logits_s, cv_t,
                         q_s.reshape(bs, *hd), k_s.reshape(bs, *hd), v_s.reshape(bs, *hd))
    a_s = a_s.reshape(bs, D_ATTN).astype(BF16)

    state_t = state_pool[0].transpose(1, 0, 2)
    pooled_s = _pool_sample(state_t, u_s)
    h1_s, h1b_s = _mix(pooled_s, a_s, gates_s, xs, w_pool_b, pscale, w_ao_b, w_o_b, g1, b1, bs, False)
    y_p, y_s = _ffn(h1b_p, h1_p, h1b_s, h1_s, w_fi_b, w_fo_b, g2, b2, 1024, tf)

    new_pool_p = u_p[n - POOL_STATE:][None, None]
    new_pool_s = jnp.concatenate([state_t[1:], u_s[None]], axis=0).transpose(1, 0, 2)[None]
    k_out = kT_p.reshape(*hd, n).transpose(2, 0, 1)[None, None]
    v_out = vT_p.reshape(*hd, n).transpose(2, 0, 1)[None, None]
    return (y_p[None], y_s[:, None], k_out, v_out,
            k_s.reshape(1, bs, 1, *hd), v_s.reshape(1, bs, 1, *hd),
            new_pool_p, new_pool_s)
```

```python
import functools
import math

import numpy as np
import jax
import jax.numpy as jnp
from jax import lax
from jax.experimental import pallas as pl
from jax.experimental.pallas import tpu as pltpu

D_MODEL = 2048
N_HEADS = 16
HEAD_DIM = 64
D_ATTN = N_HEADS * HEAD_DIM
MOBA_BLOCK = 256
MOBA_TOPK = 3
POOL_WINDOWS = (2, 4, 8, 16)
N_POOL_GROUPS = len(POOL_WINDOWS)
POOL_GROUP_DIM = 256
D_POOL = N_POOL_GROUPS * POOL_GROUP_DIM
POOL_OUT_GROUP = D_MODEL // N_POOL_GROUPS
POOL_STATE = max(POOL_WINDOWS) - 1
POOL_HALO = 16
N_BUCKETS = 32
MAX_DISTANCE = 128
D_FF = -(-8 * D_MODEL // (3 * 256)) * 256
ALPHA = 2.0 ** 0.25
LN_EPS = 1e-5
NEG_BIG = -1e30
POS_BIG = 1e30
LANES = 128
V_ROWS = 80
ATTN_UNROLL = 4
ATTN_HEADS = 4
MAX_SHIFT_EXCESS = 60.0
LOG2E = 1.4426950408889634
SCAN_BLOCKS = 8
FFN_SUB_ROWS = 512
MIX_SUB_ROWS = 256
PROJ_ROWS, PROJ_COLS = 1024, 1024
POOL_ROWS = 512
MIX_ROWS = 512
FFN_ROWS, FFN_CHUNK = 1024, 512
VMEM_LIMIT = 56 * 1024 * 1024

F32 = jnp.float32
BF16 = jnp.bfloat16


def _t5_bucket_np(rel):
    n = np.maximum(rel, 0)
    max_exact = N_BUCKETS // 2
    nf = np.maximum(n, 1).astype(np.float32)
    large = max_exact + (np.log(nf / max_exact) / math.log(MAX_DISTANCE / max_exact)
                         * (N_BUCKETS - max_exact)).astype(np.int32)
    large = np.minimum(large, N_BUCKETS - 1)
    return np.where(n < max_exact, n, large)


_BUCKETS = _t5_bucket_np(np.arange(0, 4 * MOBA_BLOCK))
BUCKET_START = tuple(int(np.argmax(_BUCKETS >= t)) for t in range(N_BUCKETS))
FAR_REL = BUCKET_START[N_BUCKETS - 1]
assert FAR_REL <= MOBA_BLOCK


def _params(*sem):
    return pltpu.CompilerParams(dimension_semantics=sem, vmem_limit_bytes=VMEM_LIMIT)


def _layer_norm(y, g, b):
    mu = jnp.mean(y, axis=-1, keepdims=True)
    d = y - mu
    var = jnp.mean(d * d, axis=-1, keepdims=True)
    return d * lax.rsqrt(var + LN_EPS) * g + b


def _bias_tile(relb_ref, h, rel):
    b = jnp.full(rel.shape, relb_ref[0, h], F32)
    for t in range(1, N_BUCKETS):
        b = jnp.where(rel >= BUCKET_START[t], relb_ref[t, h], b)
    return b


def _top_blocks(score, idx, n_idx, axis):
    chosen = jnp.zeros(score.shape, F32)
    firsts = []
    left = score
    for _ in range(MOBA_TOPK):
        mx = jnp.max(left, axis=axis, keepdims=True)
        first = jnp.min(jnp.where(left == mx, idx, float(n_idx)), axis=axis, keepdims=True)
        hit = idx == first
        chosen = jnp.where(hit, 1.0, chosen)
        left = jnp.where(hit, -jnp.inf, left)
        firsts.append(first)
    return chosen, firsts


def _proj_kernel(x_ref, w_ref, *out_refs, epilogue):
    z = jnp.dot(x_ref[...], w_ref[...], preferred_element_type=F32)
    epilogue(z, *out_refs)


def _proj(x, wb, col0, n_cols, tm, tn, outs, epilogue):
    m, k = x.shape
    assert m % tm == 0 and n_cols % tn == 0 and col0 % tn == 0
    return pl.pallas_call(
        functools.partial(_proj_kernel, epilogue=epilogue),
        grid=(n_cols // tn, m // tm),
        in_specs=[pl.BlockSpec((tm, k), lambda j, i: (i, 0)),
                  pl.BlockSpec((k, tn), lambda j, i: (0, col0 // tn + j))],
        out_specs=[o[1] for o in outs],
        out_shape=[o[0] for o in outs],
        compiler_params=_params("arbitrary", "arbitrary"),
    )(x, wb)


def _rowmajor_out(m, n_cols, tm, tn, dtype):
    return (jax.ShapeDtypeStruct((m, n_cols), dtype), pl.BlockSpec((tm, tn), lambda j, i: (i, j)))


def _transposed_out(m, n_cols, tm, tn, dtype):
    return (jax.ShapeDtypeStruct((n_cols, m), dtype), pl.BlockSpec((tn, tm), lambda j, i: (j, i)))


def _ep_f32(z, o_ref):
    o_ref[...] = z


def _ep_q_scaled(z, o_ref):
    o_ref[...] = z * (1.0 / math.sqrt(HEAD_DIM))


def _ep_qT(z, oT_ref):
    oT_ref[...] = (z * (1.0 / math.sqrt(HEAD_DIM))).T.astype(BF16)


def _ep_k(z, kT_ref, kb_ref, km_ref):
    tm, tn = z.shape
    kT_ref[...] = z.T
    zb = z.astype(BF16)
    for h in range(tn // HEAD_DIM):
        kb_ref[h] = zb[:, h * HEAD_DIM:(h + 1) * HEAD_DIM]
    km_ref[0] = jnp.mean(z.reshape(tm // MOBA_BLOCK, MOBA_BLOCK, tn), axis=1)


def _ep_v(z, vT_ref, vb_ref):
    tm, tn = z.shape
    zT = z.T
    vT_ref[...] = zT
    zTb = zT.astype(BF16)
    pad_rows = V_ROWS - HEAD_DIM
    ones_row = jnp.where(lax.broadcasted_iota(jnp.int32, (pad_rows, MOBA_BLOCK), 0) == 0,
                         1.0, 0.0).astype(BF16)
    for h in range(tn // HEAD_DIM):
        for jb in range(tm // MOBA_BLOCK):
            vb_ref[h, jb, 0:HEAD_DIM, :] = zTb[h * HEAD_DIM:(h + 1) * HEAD_DIM,
                                               jb * MOBA_BLOCK:(jb + 1) * MOBA_BLOCK]
            vb_ref[h, jb, HEAD_DIM:V_ROWS, :] = ones_row


def _ep_gate(z, o_ref):
    o_ref[...] = jax.nn.sigmoid(z).astype(BF16)


def _pool_prompt_kernel(u_ref, halo_ref, o_ref):
    i = pl.program_id(0)
    tm = u_ref.shape[0]
    u = u_ref[...]
    halo = jnp.where(i > 0, halo_ref[...], 0.0)
    ext = jnp.concatenate([halo, u], axis=0)
    pos = i * tm + lax.broadcasted_iota(jnp.int32, (tm, 1), 0)
    for g, w in enumerate(POOL_WINDOWS):
        cols = slice(g * POOL_GROUP_DIM, (g + 1) * POOL_GROUP_DIM)
        s = ext[:, cols]
        span = 1
        while span < w:
            s = s + pltpu.roll(s, span, axis=0)
            span *= 2
        cnt = jnp.minimum(pos + 1, w).astype(F32)
        o_ref[:, cols] = (s[POOL_HALO:] / cnt - u[:, cols]).astype(o_ref.dtype)


def _pool_prompt(u, tm):
    m = u.shape[0]
    hb = tm // POOL_HALO
    return pl.pallas_call(
        _pool_prompt_kernel,
        grid=(m // tm,),
        in_specs=[pl.BlockSpec((tm, D_POOL), lambda i: (i, 0)),
                  pl.BlockSpec((POOL_HALO, D_POOL), lambda i: (jnp.maximum(i * hb - 1, 0), 0))],
        out_specs=pl.BlockSpec((tm, D_POOL), lambda i: (i, 0)),
        out_shape=jax.ShapeDtypeStruct((m, D_POOL), BF16),
        compiler_params=_params("arbitrary"),
    )(u, u)


def _pool_sample_kernel(st_ref, u_ref, o_ref):
    u = u_ref[...]
    for g, w in enumerate(POOL_WINDOWS):
        cols = slice(g * POOL_GROUP_DIM, (g + 1) * POOL_GROUP_DIM)
        win = u[:, cols]
        for t in range(POOL_STATE - (w - 1), POOL_STATE):
            win = win + st_ref[t, :, cols]
        o_ref[:, cols] = (win / float(w) - u[:, cols]).astype(o_ref.dtype)


def _pool_sample(state_t, u):
    b = u.shape[0]
    return pl.pallas_call(
        _pool_sample_kernel,
        out_shape=jax.ShapeDtypeStruct((b, D_POOL), BF16),
        compiler_params=pltpu.CompilerParams(vmem_limit_bytes=VMEM_LIMIT),
    )(state_t, u)


def _mix_kernel(pooled_ref, a_ref, gp_ref, ga_ref, x_ref, wpool_ref, pscale_ref, wao_ref,
                wo_ref, g_ref, b_ref, h_ref, hb_ref, *, a_transposed, sub_rows):
    for r in range(h_ref.shape[0] // sub_rows):
        rows = slice(r * sub_rows, (r + 1) * sub_rows)
        parts = [jnp.dot(pooled_ref[rows, g * POOL_GROUP_DIM:(g + 1) * POOL_GROUP_DIM], wpool_ref[g],
                         preferred_element_type=F32) for g in range(N_POOL_GROUPS)]
        p_br = jnp.concatenate(parts, axis=-1) * pscale_ref[...]
        if a_transposed:
            a_br = lax.dot_general(a_ref[:, rows], wao_ref[...], (((0,), (0,)), ((), ())),
                                   preferred_element_type=F32)
        else:
            a_br = jnp.dot(a_ref[rows, :], wao_ref[...], preferred_element_type=F32)
        merged = gp_ref[rows, :].astype(F32) * p_br + ga_ref[rows, :].astype(F32) * a_br
        mix = jnp.dot(merged.astype(BF16), wo_ref[...], preferred_element_type=F32)
        h = _layer_norm(ALPHA * x_ref[rows, :] + mix, g_ref[...], b_ref[...])
        h_ref[rows, :] = h
        hb_ref[rows, :] = h.astype(BF16)


def _mix(pooled, a, gates, x, wpool, pscale, wao, wo, ln_g, ln_b, tm, a_transposed):
    m = x.shape[0]
    row = lambda i: (i, 0)
    const2 = lambda i: (0, 0)
    a_spec = (pl.BlockSpec((D_ATTN, tm), lambda i: (0, i)) if a_transposed
              else pl.BlockSpec((tm, D_ATTN), row))
    once = pl.Buffered(1)
    return pl.pallas_call(
        functools.partial(_mix_kernel, a_transposed=a_transposed, sub_rows=math.gcd(tm, MIX_SUB_ROWS)),
        grid=(m // tm,),
        in_specs=[pl.BlockSpec((tm, D_POOL), row),
                  a_spec,
                  pl.BlockSpec((tm, D_MODEL), lambda i: (i, 0)),
                  pl.BlockSpec((tm, D_MODEL), lambda i: (i, 1)),
                  pl.BlockSpec((tm, D_MODEL), row),
                  pl.BlockSpec((N_POOL_GROUPS, POOL_GROUP_DIM, POOL_OUT_GROUP), lambda i: (0, 0, 0),
                               pipeline_mode=once),
                  pl.BlockSpec((1, D_MODEL), const2),
                  pl.BlockSpec((D_ATTN, D_MODEL), const2, pipeline_mode=once),
                  pl.BlockSpec((D_MODEL, D_MODEL), const2, pipeline_mode=once),
                  pl.BlockSpec((1, D_MODEL), const2),
                  pl.BlockSpec((1, D_MODEL), const2)],
        out_specs=[pl.BlockSpec((tm, D_MODEL), row), pl.BlockSpec((tm, D_MODEL), row)],
        out_shape=[jax.ShapeDtypeStruct((m, D_MODEL), F32), jax.ShapeDtypeStruct((m, D_MODEL), BF16)],
        compiler_params=_params("arbitrary"),
    )(pooled, a, gates, gates, x, wpool, pscale, wao, wo, ln_g, ln_b)


def _swiglu_chunk(hb, wg_ref, wu_ref, wo_ref):
    gate = jnp.dot(hb, wg_ref[...], preferred_element_type=F32)
    up = jnp.dot(hb, wu_ref[...], preferred_element_type=F32)
    act = (gate * jax.nn.sigmoid(gate) * up).astype(BF16)
    return jnp.dot(act, wo_ref[...], preferred_element_type=F32)


def _ffn_kernel(hb_ref, h_hbm, hbs_ref, hs_ref, wg_ref, wu_ref, wo_ref, g_ref, b_ref, o_ref, os_ref,
                hres_ref, sem, *, sub_rows):
    i = pl.program_id(0)
    f = pl.program_id(1)
    last = pl.num_programs(1) - 1
    tm = o_ref.shape[0]
    residual_copy = pltpu.make_async_copy(h_hbm.at[pl.ds(i * tm, tm)], hres_ref, sem)

    @pl.when(f == 0)
    def _():
        residual_copy.start()
        o_ref[...] = jnp.zeros_like(o_ref)

    for r in range(tm // sub_rows):
        rows = pl.ds(r * sub_rows, sub_rows)
        o_ref[rows, :] += _swiglu_chunk(hb_ref[rows, :], wg_ref, wu_ref, wo_ref)

    @pl.when(f == last)
    def _():
        residual_copy.wait()
        o_ref[...] = _layer_norm(ALPHA * hres_ref[...] + o_ref[...], g_ref[...], b_ref[...])

    @pl.when(i == 0)
    def _():
        @pl.when(f == 0)
        def _():
            os_ref[...] = jnp.zeros_like(os_ref)

        os_ref[...] += _swiglu_chunk(hbs_ref[...], wg_ref, wu_ref, wo_ref)

        @pl.when(f == last)
        def _():
            os_ref[...] = _layer_norm(ALPHA * hs_ref[...] + os_ref[...], g_ref[...], b_ref[...])


def _ffn(hb, h, hb_s, h_s, w_in, w_out, ln_g, ln_b, tm, tf):
    m = h.shape[0]
    ms = h_s.shape[0]
    nf = D_FF // tf
    sub_rows = math.gcd(tm, FFN_SUB_ROWS)
    row = lambda i, f: (i, 0)
    const2 = lambda i, f: (0, 0)
    return pl.pallas_call(
        functools.partial(_ffn_kernel, sub_rows=sub_rows),
        grid=(m // tm, nf),
        in_specs=[pl.BlockSpec((tm, D_MODEL), row),
                  pl.BlockSpec(memory_space=pl.ANY),
                  pl.BlockSpec((ms, D_MODEL), const2),
                  pl.BlockSpec((ms, D_MODEL), const2),
                  pl.BlockSpec((D_MODEL, tf), lambda i, f: (0, f)),
                  pl.BlockSpec((D_MODEL, tf), lambda i, f: (0, nf + f)),
                  pl.BlockSpec((tf, D_MODEL), lambda i, f: (f, 0)),
                  pl.BlockSpec((1, D_MODEL), const2),
                  pl.BlockSpec((1, D_MODEL), const2)],
        out_specs=[pl.BlockSpec((tm, D_MODEL), row), pl.BlockSpec((ms, D_MODEL), const2)],
        out_shape=[jax.ShapeDtypeStruct((m, D_MODEL), F32), jax.ShapeDtypeStruct((ms, D_MODEL), F32)],
        scratch_shapes=[pltpu.VMEM((tm, D_MODEL), F32), pltpu.SemaphoreType.DMA(())],
        compiler_params=_params("arbitrary", "arbitrary"),
    )(hb, h, hb_s, h_s, w_in, w_in, w_out, ln_g, ln_b)


def _attn_kernel(pt_ref, relb_ref, qT_ref, k_ref, vT_ref, km_ref, *refs):
    n_pages = 2 * SCAN_BLOCKS
    page_refs = refs[:n_pages]
    qrep_ref, oT_ref, l_ref, selo_ref, bias_ref, sel_ref, sc_ref = refs[n_pages:]
    scan_steps = pt_ref.shape[1] // n_pages
    step = pl.program_id(0) * pl.num_programs(1) + pl.program_id(1)

    c = pl.program_id(1)
    ng, nb = k_ref.shape[:2]
    heads = [pl.program_id(0) * ng + g for g in range(ng)]
    blk = MOBA_BLOCK
    prev, own = 0, 1

    @pl.when(c == 0)
    def _():
        kk = lax.broadcasted_iota(jnp.int32, (blk, blk), 0)
        qq = lax.broadcasted_iota(jnp.int32, (blk, blk), 1)
        rel = qq - kk
        for g, h in enumerate(heads):
            bias_ref[g, prev] = _bias_tile(relb_ref, h, rel + blk)
            bias_ref[g, own] = jnp.where(rel >= 0, _bias_tile(relb_ref, h, rel), NEG_BIG)
        sc_ref[...] = jnp.zeros_like(sc_ref)

    _cache_scan_step(step % scan_steps, scan_steps, page_refs, qrep_ref, l_ref, selo_ref, sc_ref)

    qTs = [qT_ref[g] for g in range(ng)]

    jidx = lax.broadcasted_iota(jnp.int32, (nb, blk), 0).astype(F32)
    cf = c.astype(F32)
    for g in range(ng):
        km = km_ref[g]
        km_hi = km.astype(BF16)
        r1 = km - km_hi.astype(F32)
        km_mid = r1.astype(BF16)
        km_lo = (r1 - km_mid.astype(F32)).astype(BF16)
        score = (jnp.dot(km_hi, qTs[g], preferred_element_type=F32)
                 + jnp.dot(km_mid, qTs[g], preferred_element_type=F32)
                 + jnp.dot(km_lo, qTs[g], preferred_element_type=F32))
        chosen, _ = _top_blocks(jnp.where(jidx < cf, score, -jnp.inf), jidx, nb, 0)
        sel_ref[g] = jnp.where(jidx < cf, chosen, jnp.where(jidx == cf, 1.0, 0.0))

    n_far = jnp.maximum(c - 1, 0)
    trips = (n_far + ATTN_UNROLL - 1) // ATTN_UNROLL
    far_bias = [relb_ref[N_BUCKETS - 1, h] * LOG2E for h in heads]
    jp = jnp.maximum(c - 1, 0)

    def col_max(s):
        return jnp.max(s, axis=0, keepdims=True)

    def visit(s, keep, v_tile, shift, acc, top):
        top = jnp.maximum(top, jnp.where(keep, col_max(s), NEG_BIG))
        p = jnp.exp2(s - jnp.where(keep, shift, POS_BIG))
        return acc + jnp.dot(v_tile, p.astype(BF16), preferred_element_type=F32), top

    def attend(shift):
        accs = [jnp.zeros((V_ROWS, blk), F32)] * ng
        tops = [jnp.full((1, blk), NEG_BIG, F32)] * ng
        near = [(c, own, None), (jp, prev, c > 0)]
        logits = [(jnp.dot(k_ref[g, j], qTs[g], preferred_element_type=F32) + bias_ref[g, kind]) * LOG2E
                  for j, kind, _ in near for g in range(ng)]
        for i, (j, _, allowed) in enumerate(near):
            for g in range(ng):
                keep = sel_ref[g, pl.ds(j, 1), :] > 0.5
                keep = keep if allowed is None else jnp.logical_and(keep, allowed)
                accs[g], tops[g] = visit(logits[i * ng + g], keep, vT_ref[g, j], shift[g], accs[g], tops[g])

        def body(t, carry):
            accs, tops = list(carry[0]), list(carry[1])
            tiles = [(t * ATTN_UNROLL + u, g) for u in range(ATTN_UNROLL) for g in range(ng)]
            logits = [jnp.dot(k_ref[g, j], qTs[g], preferred_element_type=F32) * LOG2E for j, g in tiles]
            for (j, g), s in zip(tiles, logits):
                keep = jnp.logical_and(sel_ref[g, pl.ds(j, 1), :] > 0.5, j < n_far)
                accs[g], top = visit(s, keep, vT_ref[g, j], shift[g] - far_bias[g], accs[g],
                                     tops[g] - far_bias[g])
                tops[g] = top + far_bias[g]
            return tuple(accs), tuple(tops)

        return lax.fori_loop(0, trips, body, (tuple(accs), tuple(tops)))

    def write_out(accs):
        for g in range(ng):
            oT_ref[g] = (accs[g][:HEAD_DIM] / accs[g][HEAD_DIM:HEAD_DIM + 1]).astype(oT_ref.dtype)

    accs, tops = attend([jnp.zeros((1, blk), F32)] * ng)
    write_out(accs)
    off = jnp.abs(tops[0])
    for g in range(1, ng):
        off = jnp.maximum(off, jnp.abs(tops[g]))

    @pl.when(jnp.max(off) > MAX_SHIFT_EXCESS)
    def _():
        accs, _ = attend(tops)
        write_out(accs)


def _attn_prompt_and_cache_scan(rel_bias, qT, k4, vT4, km3, page_table, cache_t, q_rep):
    nh, _, n = qT.shape
    nb = n // MOBA_BLOCK
    ng = ATTN_HEADS
    assert nb % ATTN_UNROLL == 0 and nh % ng == 0
    bs, pages_per_seq = page_table.shape
    page = cache_t.shape[-1]
    assert MOBA_BLOCK == 2 * page and page == LANES
    nbs = pages_per_seq // 2
    pps = 2 * SCAN_BLOCKS
    scan_steps = pages_per_seq // pps
    assert nbs % SCAN_BLOCKS == 0 and MOBA_TOPK <= SCAN_BLOCKS and nbs <= LANES
    assert (nh // ng) * nb == bs * scan_steps

    def scan_pos(h, c):
        step = h * nb + c
        return step // scan_steps, step % scan_steps

    def page_map(h, c, pt, i):
        seq, t = scan_pos(h, c)
        return pt[seq, t * pps + i], 0, 0, 0

    page_specs = [pl.BlockSpec((1, N_HEADS, HEAD_DIM, page), functools.partial(page_map, i=i))
                  for i in range(pps)]
    return pl.pallas_call(
        _attn_kernel,
        grid_spec=pltpu.PrefetchScalarGridSpec(
            num_scalar_prefetch=1,
            grid=(nh // ng, nb),
            in_specs=[pl.BlockSpec(memory_space=pltpu.SMEM),
                      pl.BlockSpec((ng, HEAD_DIM, MOBA_BLOCK), lambda h, c, pt: (h, 0, c)),
                      pl.BlockSpec((ng, nb, MOBA_BLOCK, HEAD_DIM), lambda h, c, pt: (h, 0, 0, 0)),
                      pl.BlockSpec((ng, nb, V_ROWS, MOBA_BLOCK), lambda h, c, pt: (h, 0, 0, 0)),
                      pl.BlockSpec((ng, nb, HEAD_DIM), lambda h, c, pt: (h, 0, 0))]
                     + page_specs
                     + [pl.BlockSpec((1, N_HEADS, HEAD_DIM, page),
                                     lambda h, c, pt: (scan_pos(h, c)[0], 0, 0, 0))],
            out_specs=[pl.BlockSpec((ng, HEAD_DIM, MOBA_BLOCK), lambda h, c, pt: (h, 0, c)),
                       pl.BlockSpec((1, 1, N_HEADS, SCAN_BLOCKS * MOBA_BLOCK),
                                    lambda h, c, pt: (*scan_pos(h, c), 0, 0)),
                       pl.BlockSpec((1, N_HEADS, LANES), lambda h, c, pt: (scan_pos(h, c)[0], 0, 0))],
            scratch_shapes=[pltpu.VMEM((ng, 2, MOBA_BLOCK, MOBA_BLOCK), F32),
                            pltpu.VMEM((ng, nb, MOBA_BLOCK), F32),
                            pltpu.VMEM((N_HEADS, LANES), F32)]),
        out_shape=[jax.ShapeDtypeStruct((nh, HEAD_DIM, n), BF16),
                   jax.ShapeDtypeStruct((bs, scan_steps, N_HEADS, SCAN_BLOCKS * MOBA_BLOCK), F32),
                   jax.ShapeDtypeStruct((bs, N_HEADS, LANES), jnp.int32)],
        compiler_params=_params("arbitrary", "arbitrary"),
    )(page_table, rel_bias, qT, k4, vT4, km3, *([cache_t] * pps), q_rep)


def _cache_scan_step(t, nt, page_refs, q_ref, l_ref, sel_ref, sc_ref):
    lane = lax.broadcasted_iota(jnp.int32, sc_ref.shape, 1)
    n_seen = (t + 1) * SCAN_BLOCKS
    page = page_refs[0].shape[-1]
    for h in range(q_ref.shape[1]):
        qh = q_ref[0, h]
        for i, page_ref in enumerate(page_refs):
            l_ref[0, 0, pl.ds(h, 1), i * page:(i + 1) * page] = jnp.sum(page_ref[0, h] * qh, axis=0,
                                                                        keepdims=True)
    sc = sc_ref[...]
    for jj in range(SCAN_BLOCKS):
        logit = l_ref[0, 0, :, jj * MOBA_BLOCK:(jj + 1) * MOBA_BLOCK]
        mean = jnp.sum(logit, axis=-1, keepdims=True) * (1.0 / MOBA_BLOCK)
        sc = jnp.where(lane == t * SCAN_BLOCKS + jj, mean, sc)
    sc_ref[...] = sc
    _, firsts = _top_blocks(jnp.where(lane < n_seen, sc, -jnp.inf), lane.astype(F32), LANES, 1)
    out = jnp.zeros(sc.shape, F32)
    for r, first in enumerate(firsts):
        out = jnp.where(lane == r, first, out)
    sel_ref[0] = out.astype(jnp.int32)


def _decode_attend_kernel(sel_ref, pt_ref, relb_ref, l_ref, relbt_ref, q_ref, kn_ref, vn_ref, cv_hbm,
                          a_ref, vbuf, sem, rows_ref, blast_ref):
    s = pl.program_id(0)
    ns = pl.num_programs(0)
    nh = l_ref.shape[2]
    nb = l_ref.shape[1] * SCAN_BLOCKS
    page = vbuf.shape[-1]
    n_tiles = 2 * MOBA_TOPK
    slot = s % 2

    def tile_copy(seq, h, t, buf_slot):
        blk = sel_ref[seq, h * MOBA_TOPK + t // 2]
        return pltpu.make_async_copy(cv_hbm.at[pt_ref[seq, 2 * blk + t % 2], h],
                                     vbuf.at[buf_slot, h, t], sem.at[buf_slot])

    def start_fetch(seq, buf_slot):
        def per_head(h, carry):
            for t in range(n_tiles):
                tile_copy(seq, h, t, buf_slot).start()
            return carry
        lax.fori_loop(0, nh, per_head, 0)

    @pl.when(s == 0)
    def _():
        start_fetch(0, 0)
        pos = lax.broadcasted_iota(jnp.int32, (1, MOBA_BLOCK), 1)
        for h in range(nh):
            blast_ref[pl.ds(h, 1), :] = _bias_tile(relb_ref, h, MOBA_BLOCK - pos)

    @pl.when(s + 1 < ns)
    def _():
        start_fetch(s + 1, 1 - slot)

    for h in range(nh):
        for r in range(MOBA_TOPK):
            blk = sel_ref[s, h * MOBA_TOPK + r]
            wide = l_ref[0, blk // SCAN_BLOCKS, pl.ds(h, 1), :]
            row = wide[:, :MOBA_BLOCK]
            for seg in range(1, SCAN_BLOCKS):
                row = jnp.where(blk % SCAN_BLOCKS == seg, wide[:, seg * MOBA_BLOCK:(seg + 1) * MOBA_BLOCK], row)
            bias =jnp.where(blk == nb - 1, blast_ref[pl.ds(h, 1), :], relb_ref[N_BUCKETS - 1, h])
            rows_ref[r, pl.ds(h, 1), :] = row + bias
    q = q_ref[0]
    s_own = jnp.sum(q * kn_ref[0], axis=-1, keepdims=True) + relbt_ref[:, 0:1]
    m = s_own
    for r in range(MOBA_TOPK):
        m = jnp.maximum(m, jnp.max(rows_ref[r], axis=-1, keepdims=True))
    w_own = jnp.exp(s_own - m)
    ps = [jnp.exp(rows_ref[r] - m) for r in range(MOBA_TOPK)]
    den = w_own
    for p in ps:
        den = den + jnp.sum(p, axis=-1, keepdims=True)
    p_all = jnp.concatenate(ps, axis=-1).astype(BF16)
    num_own = w_own * vn_ref[0]

    def wait_head(h, carry):
        for t in range(n_tiles):
            tile_copy(s, h, t, slot).wait()
        return carry
    lax.fori_loop(0, nh, wait_head, 0)

    for h in range(nh):
        vt = jnp.concatenate([vbuf[slot, h, t] for t in range(n_tiles)], axis=-1).astype(BF16)
        ph = jnp.broadcast_to(p_all[h:h + 1], (8, n_tiles * page))
        pv = lax.dot_general(ph, vt, (((1,), (1,)), ((), ())), preferred_element_type=F32)
        a_ref[0, pl.ds(h, 1), :] = (pv[0:1] + num_own[h:h + 1]) / den[h:h + 1]


def _decode_attend(sel2, page_table, rel_bias, logits, cache_vt, q3, kn3, vn3):
    b, n_groups, nh, width = logits.shape
    page = cache_vt.shape[-1]
    tok = pl.BlockSpec((1, nh, HEAD_DIM), lambda s, sel, pt: (s, 0, 0))
    return pl.pallas_call(
        _decode_attend_kernel,
        grid_spec=pltpu.PrefetchScalarGridSpec(
            num_scalar_prefetch=2,
            grid=(b,),
            in_specs=[pl.BlockSpec(memory_space=pltpu.SMEM),
                      pl.BlockSpec((1, n_groups, nh, width), lambda s, sel, pt: (s, 0, 0, 0)),
                      pl.BlockSpec((nh, N_BUCKETS), lambda s, sel, pt: (0, 0)),
                      tok, tok, tok,
                      pl.BlockSpec(memory_space=pl.ANY)],
            out_specs=tok,
            scratch_shapes=[pltpu.VMEM((2, nh, 2 * MOBA_TOPK, HEAD_DIM, page), F32),
                            pltpu.SemaphoreType.DMA((2,)),
                            pltpu.VMEM((MOBA_TOPK, nh, MOBA_BLOCK), F32),
                            pltpu.VMEM((nh, MOBA_BLOCK), F32)]),
        out_shape=jax.ShapeDtypeStruct((b, nh, HEAD_DIM), F32),
        compiler_params=_params("arbitrary"),
    )(sel2, page_table, rel_bias, logits, rel_bias.T, q3, kn3, vn3, cache_vt)


def kernel(x_prompt, x_sample, cache_k, cache_v, state_pool, page_table, rel_bias, w_in, w_pool,
           pool_scale, w_attn_out, w_o, ln1_g, ln1_b, w_ffn_in, w_ffn_out, ln2_g, ln2_b):
    assert w_in.shape[0] == 1 and x_prompt.shape[0] == 1 and x_sample.shape[1] == 1
    n = x_prompt.shape[1]
    bs = x_sample.shape[0]
    nb = n // MOBA_BLOCK
    page = cache_k.shape[2]
    past_len = page_table.shape[1] * page
    assert past_len % MOBA_BLOCK == 0 and past_len >= max(POOL_WINDOWS)

    w_in_b = w_in[0].astype(BF16)
    w_pool_b = w_pool[0].astype(BF16)
    w_ao_b = w_attn_out[0].astype(BF16)
    w_o_b = w_o[0].astype(BF16)
    w_fi_b = w_ffn_in[0].astype(BF16)
    w_fo_b = w_ffn_out[0].astype(BF16)
    pscale = pool_scale[0][None]
    g1, b1, g2, b2 = ln1_g[0][None], ln1_b[0][None], ln2_g[0][None], ln2_b[0][None]

    xp = x_prompt[0]
    xs = x_sample[:, 0]
    xp_b = xp.astype(BF16)
    xs_b = xs.astype(BF16)
    col_q, col_k, col_v, col_g = D_POOL, D_POOL + D_ATTN, D_POOL + 2 * D_ATTN, D_POOL + 3 * D_ATTN
    hd = (N_HEADS, HEAD_DIM)

    tm, tn = PROJ_ROWS, PROJ_COLS
    (u_p,) = _proj(xp_b, w_in_b, 0, D_POOL, tm, tn, [_rowmajor_out(n, D_POOL, tm, tn, F32)], _ep_f32)
    (qT_p,) = _proj(xp_b, w_in_b, col_q, D_ATTN, tm, tn,
                    [_transposed_out(n, D_ATTN, tm, tn, BF16)], _ep_qT)
    kT_p, kb_p, km_p = _proj(
        xp_b, w_in_b, col_k, D_ATTN, tm, tn,
        [_transposed_out(n, D_ATTN, tm, tn, F32),
         (jax.ShapeDtypeStruct((N_HEADS, n, HEAD_DIM), BF16),
          pl.BlockSpec((N_HEADS, tm, HEAD_DIM), lambda j, i: (j, i, 0))),
         (jax.ShapeDtypeStruct((n // tm, tm // MOBA_BLOCK, D_ATTN), F32),
          pl.BlockSpec((1, tm // MOBA_BLOCK, tn), lambda j, i: (i, 0, j)))],
        _ep_k)
    vT_p, vb_p = _proj(
        xp_b, w_in_b, col_v, D_ATTN, tm, tn,
        [_transposed_out(n, D_ATTN, tm, tn, F32),
         (jax.ShapeDtypeStruct((N_HEADS, nb, V_ROWS, MOBA_BLOCK), BF16),
          pl.BlockSpec((N_HEADS, tm // MOBA_BLOCK, V_ROWS, MOBA_BLOCK), lambda j, i: (j, i, 0, 0)))],
        _ep_v)
    (gates_p,) = _proj(xp_b, w_in_b, col_g, 2 * D_MODEL, tm, tn,
                       [_rowmajor_out(n, 2 * D_MODEL, tm, tn, BF16)], _ep_gate)

    row32 = lambda nc, dt: [_rowmajor_out(bs, nc, bs, tn, dt)]
    (u_s,) = _proj(xs_b, w_in_b, 0, D_POOL, bs, tn, row32(D_POOL, F32), _ep_f32)
    (q_s,) = _proj(xs_b, w_in_b, col_q, D_ATTN, bs, tn, row32(D_ATTN, F32), _ep_q_scaled)
    (k_s,) = _proj(xs_b, w_in_b, col_k, D_ATTN, bs, tn, row32(D_ATTN, F32), _ep_f32)
    (v_s,) = _proj(xs_b, w_in_b, col_v, D_ATTN, bs, tn, row32(D_ATTN, F32), _ep_f32)
    (gates_s,) = _proj(xs_b, w_in_b, col_g, 2 * D_MODEL, bs, tn, row32(2 * D_MODEL, BF16), _ep_gate)

    ck_t = cache_k[0].transpose(0, 2, 3, 1)
    cv_t = cache_v[0].transpose(0, 2, 3, 1)
    q_rep = jnp.broadcast_to(q_s.reshape(bs, *hd, 1), (bs, *hd, page))
    km3 = km_p.reshape(nb, N_HEADS, HEAD_DIM).transpose(1, 0, 2)
    aT, logits_s, sel_s = _attn_prompt_and_cache_scan(
        rel_bias, qT_p.reshape(N_HEADS, HEAD_DIM, n), kb_p.reshape(N_HEADS, nb, MOBA_BLOCK, HEAD_DIM),
        vb_p, km3, page_table, ck_t, q_rep)

    pooled_p = _pool_prompt(u_p, POOL_ROWS)
    h1_p, h1b_p = _mix(pooled_p, aT.reshape(D_ATTN, n), gates_p, xp, w_pool_b, pscale, w_ao_b, w_o_b,
                       g1, b1, MIX_ROWS, True)

    sel2 = sel_s[:, :, :MOBA_TOPK].reshape(bs, N_HEADS * MOBA_TOPK)
    a_s = _decode_attend(sel2, page_table, rel_bias, logits_s, cv_t,
                         q_s.reshape(bs, *hd), k_s.reshape(bs, *hd), v_s.reshape(bs, *hd))
    a_s = a_s.reshape(bs, D_ATTN).astype(BF16)

    state_t = state_pool[0].transpose(1, 0, 2)
    pooled_s = _pool_sample(state_t, u_s)
    h1_s, h1b_s = _mix(pooled_s, a_s, gates_s, xs, w_pool_b, pscale, w_ao_b, w_o_b, g1, b1, bs, False)
    y_p, y_s = _ffn(h1b_p, h1_p, h1b_s, h1_s, w_fi_b, w_fo_b, g2, b2, FFN_ROWS, FFN_CHUNK)

    new_pool_p = u_p[n - POOL_STATE:][None, None]
    new_pool_s = jnp.concatenate([state_t[1:], u_s[None]], axis=0).transpose(1, 0, 2)[None]
    k_out = kT_p.reshape(*hd, n).transpose(2, 0, 1)[None, None]
    v_out = vT_p.reshape(*hd, n).transpose(2, 0, 1)[None, None]
    return (y_p[None], y_s[:, None], k_out, v_out,
            k_s.reshape(1, bs, 1, *hd), v_s.reshape(1, bs, 1, *hd),
            new_pool_p, new_pool_s)
```
